```python
import math
import jax
import jax.numpy as jnp
from jax import lax
import numpy as np

D_MODEL = 2048
BATCH = 8
SEQ = 4096
DEPTH = 4

PLE_DIM = 256
CHUNK = 64
NORM_EPS = 1e-6
N_EVEN = (DEPTH + 1) // 2
N_ODD = DEPTH // 2

RWKV_HEAD_DIM = 64
RWKV_WIDTH = D_MODEL // 2
RWKV_HEADS = RWKV_WIDTH // RWKV_HEAD_DIM
DECAY_RANK = 64
ICL_RANK = 64
VRES_RANK = 32
RWKV_GN_EPS = 64e-5

HGRN_HEAD_DIM = 128
HGRN_WIDTH = D_MODEL // 2
HGRN_HEADS = HGRN_WIDTH // HGRN_HEAD_DIM

GDN_HEAD_DIM = 128
GDN_QK_HEADS = D_MODEL // GDN_HEAD_DIM
GDN_V_HEADS = 2 * GDN_QK_HEADS
GDN_QK_WIDTH = GDN_QK_HEADS * GDN_HEAD_DIM
GDN_V_WIDTH = GDN_V_HEADS * GDN_HEAD_DIM
GDN_CONV = 4
GDN_CONV_DIM = 2 * GDN_QK_WIDTH + GDN_V_WIDTH

RWKV_COLS = 4 * RWKV_WIDTH + DECAY_RANK + ICL_RANK
HGRN_COLS = 4 * HGRN_WIDTH
EVEN_COLS = RWKV_COLS + HGRN_COLS
EVEN_MIX = RWKV_WIDTH + HGRN_WIDTH
ODD_COLS = GDN_CONV_DIM + GDN_V_WIDTH + 2 * GDN_V_HEADS

F32 = jnp.float32

kernel_name = 'hybrid_rwkv7_hgrn2_gdn_trunk'


def rms_norm(h, g, eps=NORM_EPS):
    hf = h.astype(F32)
    hf = hf * lax.rsqrt(jnp.mean(hf * hf, axis=-1, keepdims=True) + eps)
    return (hf * g.astype(F32)).astype(h.dtype)


def l2_normalize(h, eps=1e-6):
    hf = h.astype(F32)
    return hf * lax.rsqrt(jnp.sum(hf * hf, axis=-1, keepdims=True) + eps)


def token_shift(h, mu):
    prev = jnp.pad(h, ((0, 0), (1, 0), (0, 0)))[:, :-1]
    return h + (prev - h) * mu


def causal_depthwise_conv(h, w):
    return lax.conv_general_dilated(h, w[:, None, :].astype(h.dtype), window_strides=(1,),
                                    padding=[(w.shape[0] - 1, 0)],
                                    dimension_numbers=('NWC', 'WIO', 'NWC'),
                                    feature_group_count=h.shape[-1])


def to_chunks(t):
    b, n = t.shape[:2]
    t = t.reshape((b, n // CHUNK, CHUNK) + t.shape[2:])
    return jnp.moveaxis(jnp.moveaxis(t, 1, 0), 3, 2)


def from_chunks(t):
    nc, b, h, c, d = t.shape
    return jnp.moveaxis(jnp.moveaxis(t, 2, 3), 0, 1).reshape(b, nc * c, h, d)


def chunk_masks():
    idx = jnp.arange(CHUNK)
    return idx[:, None] >= idx[None, :], idx[:, None] > idx[None, :]


def rwkv7_recurrence(r, logw, k, v, a, b):
    incl, strict = chunk_masks()
    eye = jnp.eye(CHUNK, dtype=F32)
    xs = tuple(to_chunks(t.astype(F32)) for t in (r, logw, k, v, a, b))
    bsz, nh, kd, vd = r.shape[0], r.shape[2], r.shape[3], v.shape[3]

    def step(S, inp):
        r_c, lw_c, k_c, v_c, a_c, b_c = inp
        c = jnp.cumsum(lw_c, axis=2)
        c_prev = c - lw_c
        e_prev = jnp.exp(jnp.where(strict[:, :, None], c_prev[:, :, :, None] - c[:, :, None], -jnp.inf))
        e_cur = jnp.exp(jnp.where(incl[:, :, None], c[:, :, :, None] - c[:, :, None], -jnp.inf))
        ae = a_c[:, :, :, None] * e_prev
        re = r_c[:, :, :, None] * e_cur
        m_ab = jnp.einsum('bhtsc,bhsc->bhts', ae, b_c)
        m_ak = jnp.einsum('bhtsc,bhsc->bhts', ae, k_c)
        p_rb = jnp.einsum('bhtsc,bhsc->bhts', re, b_c)
        p_rk = jnp.einsum('bhtsc,bhsc->bhts', re, k_c)
        rhs = (jnp.einsum('bhtc,bhcv->bhtv', a_c * jnp.exp(c_prev), S)
               + jnp.einsum('bhts,bhsv->bhtv', m_ak, v_c))
        u = lax.linalg.triangular_solve(eye - m_ab, rhs, left_side=True, lower=True, unit_diagonal=True)
        o = (jnp.einsum('bhtc,bhcv->bhtv', r_c * jnp.exp(c), S)
             + jnp.einsum('bhts,bhsv->bhtv', p_rb, u)
             + jnp.einsum('bhts,bhsv->bhtv', p_rk, v_c))
        tail = jnp.exp(c[:, :, -1:] - c)
        S = (jnp.exp(c[:, :, -1])[..., None] * S
             + jnp.einsum('bhsc,bhsv->bhcv', b_c * tail, u)
             + jnp.einsum('bhsc,bhsv->bhcv', k_c * tail, v_c))
        return S, o

    _, o = lax.scan(step, jnp.zeros((bsz, nh, kd, vd), F32), xs)
    return from_chunks(o)


def gla_recurrence(q, logf, k, v):
    incl, _ = chunk_masks()
    xs = tuple(to_chunks(t.astype(F32)) for t in (q, logf, k, v))
    bsz, nh, kd, vd = q.shape[0], q.shape[2], q.shape[3], v.shape[3]

    def step(S, inp):
        q_c, lf_c, k_c, v_c = inp
        c = jnp.cumsum(lf_c, axis=2)
        e = jnp.exp(jnp.where(incl[:, :, None], c[:, :, :, None] - c[:, :, None], -jnp.inf))
        att = jnp.einsum('bhtsc,bhsc->bhts', q_c[:, :, :, None] * e, k_c)
        o = (jnp.einsum('bhtc,bhcv->bhtv', q_c * jnp.exp(c), S)
             + jnp.einsum('bhts,bhsv->bhtv', att, v_c))
        tail = jnp.exp(c[:, :, -1:] - c)
        S = jnp.exp(c[:, :, -1])[..., None] * S + jnp.einsum('bhsc,bhsv->bhcv', k_c * tail, v_c)
        return S, o

    _, o = lax.scan(step, jnp.zeros((bsz, nh, kd, vd), F32), xs)
    return from_chunks(o)


def gated_delta_recurrence(q, k, v, beta, logg):
    incl, strict = chunk_masks()
    eye = jnp.eye(CHUNK, dtype=F32)
    xs = tuple(to_chunks(t.astype(F32)) for t in (q, k, v, beta, logg))
    bsz, nh, kd, vd = q.shape[0], q.shape[2], q.shape[3], v.shape[3]

    def step(S, inp):
        q_c, k_c, v_c, b_c, g_c = inp
        gc = jnp.cumsum(g_c, axis=-1)
        dm = jnp.exp(jnp.where(incl, gc[..., :, None] - gc[..., None, :], -jnp.inf))
        kk = jnp.einsum('bhtc,bhsc->bhts', k_c, k_c)
        lower = jnp.where(strict, b_c[..., None] * kk * dm, 0.0)
        rhs = jnp.concatenate([b_c[..., None] * v_c, (b_c * jnp.exp(gc))[..., None] * k_c], axis=-1)
        sol = lax.linalg.triangular_solve(eye + lower, rhs, left_side=True, lower=True, unit_diagonal=True)
        new_v = sol[..., :vd] - jnp.einsum('bhtc,bhcv->bhtv', sol[..., vd:], S)
        qk = jnp.einsum('bhtc,bhsc->bhts', q_c, k_c) * dm
        o = (jnp.einsum('bhtc,bhcv->bhtv', q_c * jnp.exp(gc)[..., None], S)
             + jnp.einsum('bhts,bhsv->bhtv', qk, new_v))
        tail = jnp.exp(gc[..., -1:] - gc)[..., None]
        S = jnp.exp(gc[..., -1])[..., None, None] * S + jnp.einsum('bhsc,bhsv->bhcv', k_c * tail, new_v)
        return S, o

    _, o = lax.scan(step, jnp.zeros((bsz, nh, kd, vd), F32), xs)
    return from_chunks(o)


def even_mixer(h, w_in, mu, w0, w2, a0, a2, k_k, k_a, r_k, lnx_w, lnx_b, lb, first_hgrn,
               hg_norm_w, w_out, v_first, vres):
    bsz, n, _ = h.shape
    if vres is None:
        proj = h @ w_in
    else:
        w_vd, mu_vd, v0, v2 = vres
        proj = h @ jnp.concatenate([w_in, w_vd], axis=1)

    rw = token_shift(proj[..., :RWKV_COLS], mu)
    splits = [RWKV_WIDTH, 2 * RWKV_WIDTH, 3 * RWKV_WIDTH, 4 * RWKV_WIDTH, 4 * RWKV_WIDTH + DECAY_RANK]
    r, k, v, g_a, wd, ad = jnp.split(rw, splits, axis=-1)
    w_raw = w0.astype(F32) + jnp.tanh(wd.astype(F32)) @ w2.astype(F32)
    logw = -jnp.exp(-jax.nn.softplus(-w_raw) - 0.5)
    if vres is None:
        v_first = v
    else:
        vd = token_shift(proj[..., EVEN_COLS:], mu_vd)
        v = v + (v_first - v) * jax.nn.sigmoid(v0 + vd @ v2)
    icl = jax.nn.sigmoid(a0 + ad @ a2)
    heads = lambda t: t.reshape(bsz, n, RWKV_HEADS, RWKV_HEAD_DIM)
    kk = l2_normalize(heads(k * k_k), eps=1e-12)
    icl_h = heads(icl).astype(F32)
    k = k * (1 + (icl - 1) * k_a)
    r_h, k_h, v_h = heads(r), heads(k), heads(v)
    o = rwkv7_recurrence(r_h, heads(logw), k_h, v_h, -kk, kk * icl_h)
    mean = jnp.mean(o, axis=-1, keepdims=True)
    var = jnp.mean(jnp.square(o - mean), axis=-1, keepdims=True)
    o = ((o - mean) * lax.rsqrt(var + RWKV_GN_EPS) * lnx_w.reshape(RWKV_HEADS, RWKV_HEAD_DIM)
         + lnx_b.reshape(RWKV_HEADS, RWKV_HEAD_DIM))
    bonus = jnp.sum(r_h * k_h * r_k, axis=-1, keepdims=True) * v_h
    o_a = (o + bonus).reshape(bsz, n, RWKV_WIDTH) * jax.nn.silu(g_a)

    q, f, i_in, g_b = jnp.split(proj[..., RWKV_COLS:EVEN_COLS], 4, axis=-1)
    z = f.astype(F32)
    if first_hgrn:
        log_f, k_f = jax.nn.log_sigmoid(z), jax.nn.sigmoid(-z)
    else:
        log_f = jnp.log(lb + (1 - lb) * jax.nn.sigmoid(z))
        k_f = (1 - lb) * jax.nn.sigmoid(-z)
    hh = lambda t: t.reshape(bsz, n, HGRN_HEADS, HGRN_HEAD_DIM)
    o = gla_recurrence(hh(jax.nn.silu(q)), hh(log_f), hh(k_f), hh(i_in))
    o_b = rms_norm(o, hg_norm_w).reshape(bsz, n, HGRN_WIDTH) * jax.nn.silu(g_b)

    mix = jnp.concatenate([o_a.astype(h.dtype), o_b.astype(h.dtype)], axis=-1)
    return mix @ w_out, v_first


def odd_mixer(h, w_in, conv_w, a_log, dt_bias, norm_w, w_out):
    bsz, n, _ = h.shape
    proj = h @ w_in
    qkv = jax.nn.silu(causal_depthwise_conv(proj[..., :GDN_CONV_DIM], conv_w))
    q, k, v = jnp.split(qkv, [GDN_QK_WIDTH, 2 * GDN_QK_WIDTH], axis=-1)
    z = proj[..., GDN_CONV_DIM:GDN_CONV_DIM + GDN_V_WIDTH]
    b_raw = proj[..., GDN_CONV_DIM + GDN_V_WIDTH:ODD_COLS - GDN_V_HEADS]
    a_raw = proj[..., ODD_COLS - GDN_V_HEADS:]
    rep = GDN_V_HEADS // GDN_QK_HEADS
    q = jnp.repeat(l2_normalize(q.reshape(bsz, n, GDN_QK_HEADS, GDN_HEAD_DIM)), rep, axis=2) * GDN_HEAD_DIM ** -0.5
    k = jnp.repeat(l2_normalize(k.reshape(bsz, n, GDN_QK_HEADS, GDN_HEAD_DIM)), rep, axis=2)
    v = v.reshape(bsz, n, GDN_V_HEADS, GDN_HEAD_DIM)
    beta = jax.nn.sigmoid(b_raw.astype(F32))
    logg = -jnp.exp(a_log.astype(F32)) * jax.nn.softplus(a_raw.astype(F32) + dt_bias.astype(F32))
    o = gated_delta_recurrence(q, k, v, beta, logg)
    o = rms_norm(o, norm_w) * jax.nn.silu(z.reshape(bsz, n, GDN_V_HEADS, GDN_HEAD_DIM).astype(F32))
    return o.reshape(bsz, n, GDN_V_WIDTH).astype(h.dtype) @ w_out


def setup_inputs(seed: int = 0) -> dict:
    key = jax.random.key(seed)
    ks = iter(jax.random.split(key, 32))
    nrm = lambda shape, scale: jax.random.normal(next(ks), shape, F32) * scale
    D = D_MODEL
    inp = {}
    inp['x'] = nrm((BATCH, SEQ, D), 1.0)
    inp['p'] = nrm((DEPTH, BATCH, SEQ, PLE_DIM), 1.0)
    inp['pre_g'] = 1.0 + nrm((DEPTH, D), 0.02)
    inp['post_g'] = 1.0 + nrm((DEPTH, D), 0.02)
    inp['ple_w_proj'] = nrm((DEPTH, PLE_DIM, D), PLE_DIM ** -0.5)
    inp['ple_w_gate'] = nrm((DEPTH, D, D), D ** -0.5)
    inp['ev_w_in'] = nrm((N_EVEN, D, EVEN_COLS), D ** -0.5)
    inp['ev_mu'] = jax.random.uniform(next(ks), (N_EVEN, RWKV_COLS), F32)
    inp['ev_w0'] = nrm((N_EVEN, RWKV_WIDTH), 0.5)
    inp['ev_w2'] = nrm((N_EVEN, DECAY_RANK, RWKV_WIDTH), DECAY_RANK ** -0.5)
    inp['ev_a0'] = nrm((N_EVEN, RWKV_WIDTH), 0.1)
    inp['ev_a2'] = nrm((N_EVEN, ICL_RANK, RWKV_WIDTH), ICL_RANK ** -0.5)
    inp['ev_k_k'] = 0.85 + nrm((N_EVEN, RWKV_WIDTH), 0.02)
    inp['ev_k_a'] = 1.0 + nrm((N_EVEN, RWKV_WIDTH), 0.02)
    inp['ev_r_k'] = nrm((N_EVEN, RWKV_HEADS, RWKV_HEAD_DIM), 0.1)
    inp['ev_lnx_w'] = 1.0 + nrm((N_EVEN, RWKV_WIDTH), 0.02)
    inp['ev_lnx_b'] = nrm((N_EVEN, RWKV_WIDTH), 0.01)
    inp['vr_w_down'] = nrm((N_EVEN - 1, D, VRES_RANK), D ** -0.5)
    inp['vr_mu'] = jax.random.uniform(next(ks), (N_EVEN - 1, VRES_RANK), F32)
    inp['vr_v0'] = nrm((N_EVEN - 1, RWKV_WIDTH), 0.1)
    inp['vr_v2'] = nrm((N_EVEN - 1, VRES_RANK, RWKV_WIDTH), VRES_RANK ** -0.5)
    inp['hg_lb'] = nrm((N_EVEN, HGRN_WIDTH), 1.0)
    inp['hg_norm_w'] = 1.0 + nrm((N_EVEN, HGRN_HEAD_DIM), 0.02)
    inp['ev_w_out'] = nrm((N_EVEN, EVEN_MIX, D), EVEN_MIX ** -0.5)
    inp['od_w_in'] = nrm((N_ODD, D, ODD_COLS), D ** -0.5)
    inp['od_conv_w'] = nrm((N_ODD, GDN_CONV, GDN_CONV_DIM), GDN_CONV ** -0.5)
    inp['od_a_log'] = jnp.log(jax.random.uniform(next(ks), (N_ODD, GDN_V_HEADS), F32, minval=1.0, maxval=16.0))
    dt = jnp.exp(jax.random.uniform(next(ks), (N_ODD, GDN_V_HEADS), F32)
                 * (math.log(0.1) - math.log(1e-3)) + math.log(1e-3))
    inp['od_dt_bias'] = dt + jnp.log(-jnp.expm1(-dt))
    inp['od_norm_w'] = 1.0 + nrm((N_ODD, GDN_HEAD_DIM), 0.02)
    inp['od_w_out'] = nrm((N_ODD, GDN_V_WIDTH, D), GDN_V_WIDTH ** -0.5)
    return inp


def reference(x, p, pre_g, post_g, ple_w_proj, ple_w_gate, ev_w_in, ev_mu, ev_w0, ev_w2, ev_a0, ev_a2,
              ev_k_k, ev_k_a, ev_r_k, ev_lnx_w, ev_lnx_b, vr_w_down, vr_mu, vr_v0, vr_v2, hg_lb, hg_norm_w,
              ev_w_out, od_w_in, od_conv_w, od_a_log, od_dt_bias, od_norm_w, od_w_out):
    lb_all = jnp.cumsum(jax.nn.softmax(hg_lb.astype(F32), axis=0), axis=0)
    lb_all = lb_all - lb_all[0]
    v_first = None
    for i in range(DEPTH):
        j = i // 2
        h = rms_norm(x, pre_g[i])
        if i % 2 == 0:
            vres = None if j == 0 else (vr_w_down[j - 1], vr_mu[j - 1], vr_v0[j - 1], vr_v2[j - 1])
            y, v_first = even_mixer(h, ev_w_in[j], ev_mu[j], ev_w0[j], ev_w2[j], ev_a0[j], ev_a2[j],
                                    ev_k_k[j], ev_k_a[j], ev_r_k[j], ev_lnx_w[j], ev_lnx_b[j], lb_all[j],
                                    j == 0, hg_norm_w[j], ev_w_out[j], v_first, vres)
        else:
            y = odd_mixer(h, od_w_in[j], od_conv_w[j], od_a_log[j], od_dt_bias[j], od_norm_w[j], od_w_out[j])
        x = x + rms_norm(y, post_g[i])
        x = x + jax.nn.sigmoid(x @ ple_w_gate[i]) * (p[i] @ ple_w_proj[i])
    return x
```

```python
import functools
import math

import jax
import jax.numpy as jnp
from jax import lax
from jax.experimental import pallas as pl
from jax.experimental.pallas import tpu as pltpu

F32 = jnp.float32
BF16 = jnp.bfloat16

LANES = 128
CARRY_ROWS = 8
VMEM_LIMIT_BYTES = 56 * 1024 * 1024

CHUNK = 64
SUB = 16
NORM_EPS = 1e-6
RWKV_HEAD_DIM = 64
RWKV_GN_EPS = 64e-5
DECAY_RANK = 64
ICL_RANK = 64
VRES_RANK = 32
GDN_HEAD_DIM = 128
GDN_CONV = 4


def _dot(a, b):
    return jnp.dot(a, b, preferred_element_type=F32)


def _dot_nt(a, b):
    return lax.dot_general(a, b, (((1,), (1,)), ((), ())), preferred_element_type=F32)


def _dot_tn(a, b):
    return lax.dot_general(a, b, (((0,), (0,)), ((), ())), preferred_element_type=F32)


def _split3(x):
    hi = x.astype(BF16)
    r1 = x - hi.astype(F32)
    mid = r1.astype(BF16)
    lo = (r1 - mid.astype(F32)).astype(BF16)
    return hi, mid, lo


def _cumsum_rows(tril, x):
    hi, mid, lo = _split3(x)
    return _dot(tril, hi) + _dot(tril, mid) + _dot(tril, lo)


def _colsum_bcast(x, ones):
    hi, mid, lo = _split3(x)
    return _dot_tn(hi, ones) + _dot_tn(mid, ones) + _dot_tn(lo, ones)


def _sigmoid(x):
    return 1.0 / (1.0 + jnp.exp(-x))


def _silu(x):
    return x * _sigmoid(x)


def _softplus(x):
    return jnp.maximum(x, 0.0) + jnp.log(1.0 + jnp.exp(-jnp.abs(x)))


def _neumann_inverse(n):
    size = n.shape[0]
    eye = (lax.broadcasted_iota(jnp.int32, (size, size), 0)
           == lax.broadcasted_iota(jnp.int32, (size, size), 1)).astype(F32)
    x = eye + n
    p = n
    for _ in range(int(math.log2(CHUNK)) - 1):
        pb = p.astype(BF16)
        p = _dot(pb, pb)
        x = x + _dot(x.astype(BF16), p.astype(BF16))
    return x


def _lane_masks():
    lane = lax.broadcasted_iota(jnp.int32, (1, LANES), 1)
    m0 = (lane < LANES // 2).astype(F32)
    return lane, m0, 1.0 - m0


def _pair_time_masks():
    row = lax.broadcasted_iota(jnp.int32, (CHUNK, LANES), 0)
    src = lax.broadcasted_iota(jnp.int32, (CHUNK, LANES), 1) & (CHUNK - 1)
    return src < row, src <= row


def _norm_proj_kernel(x_ref, g_ref, wm_ref, ws_ref, om_ref, os_ref, h_ref):
    @pl.when(pl.program_id(1) == 0)
    def _():
        x = x_ref[...]
        ms = jnp.mean(x * x, axis=-1, keepdims=True)
        h = (x * lax.rsqrt(ms + NORM_EPS) * g_ref[...]).astype(BF16)
        h_ref[...] = h
        os_ref[...] = _dot(h, ws_ref[...])

    om_ref[...] = _dot(h_ref[...], wm_ref[...])


def _norm_proj(x, g, wm, ws):
    m, d = x.shape
    nm, ns = wm.shape[1], ws.shape[1]
    tm = min(1024, m)
    tn = min(1024, nm)
    return pl.pallas_call(
        _norm_proj_kernel,
        grid=(m // tm, nm // tn),
        in_specs=[pl.BlockSpec((tm, d), lambda i, j: (i, 0)),
                  pl.BlockSpec((1, d), lambda i, j: (0, 0)),
                  pl.BlockSpec((d, tn), lambda i, j: (0, j)),
                  pl.BlockSpec((d, ns), lambda i, j: (0, 0))],
        out_specs=[pl.BlockSpec((tm, tn), lambda i, j: (i, j)),
                   pl.BlockSpec((tm, ns), lambda i, j: (i, 0))],
        out_shape=[jax.ShapeDtypeStruct((m, nm), F32), jax.ShapeDtypeStruct((m, ns), F32)],
        scratch_shapes=[pltpu.VMEM((tm, d), BF16)],
        compiler_params=pltpu.CompilerParams(dimension_semantics=("parallel", "arbitrary"),
                                             vmem_limit_bytes=VMEM_LIMIT_BYTES),
        name="norm_proj",
    )(x, g.reshape(1, d), wm, ws)


def _post_kernel(*refs, n_mix):
    mix_refs, w_refs = refs[:n_mix], refs[n_mix:2 * n_mix]
    x_ref, pg_ref, p_ref, wg_ref, wp_ref, o_ref = refs[2 * n_mix:]
    y = _dot(mix_refs[0][...], w_refs[0][...])
    for mr, wr in zip(mix_refs[1:], w_refs[1:]):
        y = y + _dot(mr[...], wr[...])
    ms = jnp.mean(y * y, axis=-1, keepdims=True)
    x1 = x_ref[...] + y * lax.rsqrt(ms + NORM_EPS) * pg_ref[...]
    gate = _sigmoid(_dot(x1.astype(BF16), wg_ref[...]))
    o_ref[...] = x1 + gate * _dot(p_ref[...].astype(BF16), wp_ref[...])


def _post(mixes, w_outs, x, post_g, p, w_gate, w_proj):
    m, d = x.shape
    tm = min(256, m)
    n_mix = len(mixes)
    row_spec = lambda a: pl.BlockSpec((tm, a.shape[1]), lambda i: (i, 0))
    full_spec = lambda a: pl.BlockSpec(a.shape, lambda i: (0, 0))
    return pl.pallas_call(
        functools.partial(_post_kernel, n_mix=n_mix),
        grid=(m // tm,),
        in_specs=([row_spec(a) for a in mixes] + [full_spec(w) for w in w_outs]
                  + [row_spec(x), pl.BlockSpec((1, d), lambda i: (0, 0)), row_spec(p),
                     full_spec(w_gate), full_spec(w_proj)]),
        out_specs=row_spec(x),
        out_shape=jax.ShapeDtypeStruct((m, d), F32),
        compiler_params=pltpu.CompilerParams(dimension_semantics=("parallel",),
                                             vmem_limit_bytes=VMEM_LIMIT_BYTES),
        name="post",
    )(*mixes, *w_outs, x, post_g.reshape(1, d), p, w_gate, w_proj)


def _hgrn_kernel(pm_ref, lb_ref, nw_ref, sel_ref, out_ref, s_ref, kc_ref, *, first, n_heads, width):
    L = CHUNK

    @pl.when(pl.program_id(1) == 0)
    def _():
        s_ref[...] = jnp.zeros_like(s_ref)

    row = lax.broadcasted_iota(jnp.int32, (L, L), 0)
    col = lax.broadcasted_iota(jnp.int32, (L, L), 1)
    tril = (row >= col).astype(BF16)
    row_blk, col_blk = row // SUB, col // SUB
    off_mask = col_blk < row_blk
    diag_mask = jnp.logical_and(col_blk == row_blk, col <= row)
    ones = jnp.ones((L, LANES), BF16)
    nw = nw_ref[...]
    sel = sel_ref[...]

    for h in range(n_heads):
        ls = slice(h * LANES, (h + 1) * LANES)
        qp = pm_ref[:, h * LANES:(h + 1) * LANES]
        z = pm_ref[:, width + h * LANES:width + (h + 1) * LANES]
        v = pm_ref[:, 2 * width + h * LANES:2 * width + (h + 1) * LANES]
        gb = pm_ref[:, 3 * width + h * LANES:3 * width + (h + 1) * LANES]

        e = jnp.exp(-jnp.abs(z))
        inv = 1.0 / (1.0 + e)
        pos = z >= 0.0
        sig = jnp.where(pos, inv, e * inv)
        nsig = jnp.where(pos, e * inv, inv)
        if first:
            lf = jnp.minimum(z, 0.0) - jnp.log(1.0 + e)
            k = nsig
        else:
            lb = lb_ref[:, ls]
            lf = jnp.log(lb + (1.0 - lb) * sig)
            k = (1.0 - lb) * nsig
        q = _silu(qp)
        c = _cumsum_rows(tril, lf)
        kc_ref[0] = k
        kc_ref[1] = c

        att_rows = [jnp.zeros((SUB, L), F32)]
        for i in range(1, L // SUB):
            ref_c = kc_ref[1, i * SUB:i * SUB + 1, :]
            kt = (k * jnp.exp(jnp.minimum(ref_c - c, 0.0))).astype(BF16)
            qt = (q[i * SUB:(i + 1) * SUB] * jnp.exp(c[i * SUB:(i + 1) * SUB] - ref_c)).astype(BF16)
            att_rows.append(_dot_nt(qt, kt))
        att_off = jnp.concatenate(att_rows, axis=0)

        p_rows = []
        for i in range(L // SUB):
            q_i = q[i * SUB:(i + 1) * SUB]
            c_i = c[i * SUB:(i + 1) * SUB]
            pieces = []
            for j in range(SUB):
                k_row = kc_ref[0, i * SUB + j:i * SUB + j + 1, :]
                c_row = kc_ref[1, i * SUB + j:i * SUB + j + 1, :]
                dec = jnp.exp(jnp.minimum(c_i - c_row, 0.0))
                pieces.append((q_i * dec * k_row).astype(BF16))
            p_rows.append(jnp.concatenate(pieces, axis=1))
        p_all = jnp.concatenate(p_rows, axis=0)
        att_diag = _dot(p_all, sel)[:, :L]

        att = jnp.where(off_mask, att_off, 0.0) + jnp.where(diag_mask, att_diag, 0.0)

        s = s_ref[h]
        lhs = jnp.concatenate([(q * jnp.exp(c)).astype(BF16), att.astype(BF16)], axis=1)
        rhs = jnp.concatenate([s.astype(BF16), v.astype(BF16)], axis=0)
        o = _dot(lhs, rhs)

        c_last = kc_ref[1, L - 1:L, :]
        k_tail = (k * jnp.exp(c_last - c)).astype(BF16)
        s_ref[h] = jnp.exp(_colsum_bcast(lf, ones)) * s + _dot_tn(k_tail, v.astype(BF16))

        ms = jnp.mean(o * o, axis=-1, keepdims=True)
        out_ref[:, ls] = (o * lax.rsqrt(ms + NORM_EPS) * nw * _silu(gb)).astype(BF16)


def _hgrn(proj_main, col_block, lb, norm_w, first, batch, n_chunks, width):
    m = proj_main.shape[0]
    n_heads = width // LANES
    j = jnp.arange(SUB * LANES) // LANES
    s = jnp.arange(LANES)
    sel = ((s[None, :] % SUB == j[:, None]) & (s[None, :] < CHUNK)).astype(BF16)
    return pl.pallas_call(
        functools.partial(_hgrn_kernel, first=first, n_heads=n_heads, width=width),
        grid=(batch, n_chunks),
        in_specs=[pl.BlockSpec((CHUNK, 4 * width), lambda b, c: (b * n_chunks + c, col_block)),
                  pl.BlockSpec((1, width), lambda b, c: (0, 0)),
                  pl.BlockSpec((1, LANES), lambda b, c: (0, 0)),
                  pl.BlockSpec((SUB * LANES, LANES), lambda b, c: (0, 0))],
        out_specs=pl.BlockSpec((CHUNK, width), lambda b, c: (b * n_chunks + c, 0)),
        out_shape=jax.ShapeDtypeStruct((m, width), BF16),
        scratch_shapes=[pltpu.VMEM((n_heads, LANES, LANES), F32),
                        pltpu.VMEM((2, CHUNK, LANES), F32)],
        compiler_params=pltpu.CompilerParams(dimension_semantics=("parallel", "arbitrary"),
                                             vmem_limit_bytes=VMEM_LIMIT_BYTES),
        name="hgrn2",
    )(proj_main, lb.reshape(1, width), norm_w.reshape(1, LANES), sel)


def _rwkv_kernel(*refs, has_vres, n_pairs, width):
    (pm_ref, ps_ref, mum_ref, mus_ref, w0_ref, w2_ref, a0_ref, a2_ref, kk_ref, ka_ref, rk_ref,
     lnw_ref, lnb_ref) = refs[:13]
    if has_vres:
        vf_ref, v0_ref, v2_ref, out_ref, s_ref, cm_ref, cs_ref = refs[13:]
        vfo_ref = None
    else:
        out_ref, vfo_ref, s_ref, cm_ref, cs_ref = refs[13:]
    L = CHUNK

    @pl.when(pl.program_id(1) == 0)
    def _():
        s_ref[...] = jnp.zeros_like(s_ref)
        cm_ref[...] = jnp.zeros_like(cm_ref)
        cs_ref[...] = jnp.zeros_like(cs_ref)

    row1 = lax.broadcasted_iota(jnp.int32, (L, 1), 0)

    def token_shift(x, carry_row, mu):
        prev = jnp.where(row1 == 0, carry_row, pltpu.roll(x, 1, 0))
        return x + (prev - x) * mu

    ps = ps_ref[...]
    pss = token_shift(ps, cs_ref[0:1, :], mus_ref[...])
    cs_ref[0:1, :] = ps[L - 1:L, :]
    wd_t = jnp.tanh(pss[:, 0:LANES]).astype(BF16)
    ad = pss[:, LANES:2 * LANES].astype(BF16)
    vd = pss[:, 2 * LANES:3 * LANES].astype(BF16)

    _, m0, m1 = _lane_masks()
    strict2, incl2 = _pair_time_masks()
    row = lax.broadcasted_iota(jnp.int32, (L, L), 0)
    col = lax.broadcasted_iota(jnp.int32, (L, L), 1)
    tril = (row >= col).astype(BF16)
    rr = lax.broadcasted_iota(jnp.int32, (LANES, LANES), 0) // RWKV_HEAD_DIM
    cc = lax.broadcasted_iota(jnp.int32, (LANES, LANES), 1) // RWKV_HEAD_DIM
    bd_mask = (rr == cc).astype(F32)
    g128 = bd_mask.astype(BF16)
    ones = jnp.ones((L, LANES), BF16)
    inv_hd = 1.0 / RWKV_HEAD_DIM
    decay_scale = -math.exp(-0.5)

    def stack_heads(x):
        return jnp.concatenate([x * m0, x * m1], axis=0)

    for j in range(n_pairs):
        ls = slice(j * LANES, (j + 1) * LANES)

        def load(q):
            sl = slice(q * width + j * LANES, q * width + (j + 1) * LANES)
            x = pm_ref[:, sl]
            y = token_shift(x, cm_ref[0:1, sl], mum_ref[:, sl])
            cm_ref[0:1, sl] = x[L - 1:L, :]
            return y

        r, k, v, g = load(0), load(1), load(2), load(3)
        lw = decay_scale * _sigmoid(w0_ref[:, ls] + _dot(wd_t, w2_ref[:, ls]))
        if has_vres:
            v = v + (vf_ref[:, ls] - v) * _sigmoid(v0_ref[:, ls] + _dot(vd, v2_ref[:, ls]))
        else:
            vfo_ref[:, ls] = v
        icl = _sigmoid(a0_ref[:, ls] + _dot(ad, a2_ref[:, ls]))
        kkr = k * kk_ref[:, ls]
        kkn = kkr * lax.rsqrt(_dot((kkr * kkr).astype(BF16), g128) + 1e-12)
        k2 = k * (1.0 + (icl - 1.0) * ka_ref[:, ls])
        a = -kkn
        b = kkn * icl

        c = _cumsum_rows(tril, lw)
        c_last = c[L - 1:L, :]
        e_neg = jnp.exp(-c)
        e_tail = jnp.exp(c_last - c)
        a_t = a * jnp.exp(c - lw)
        r_t = r * jnp.exp(c)
        b_t = b * e_neg
        k_t = k2 * e_neg

        lhs = jnp.concatenate([a_t, r_t], axis=0).astype(BF16)
        rhs_t = jnp.concatenate([stack_heads(b_t), stack_heads(k_t)], axis=0).astype(BF16)
        mm = _dot_nt(lhs, rhs_t)
        m_ab = jnp.where(strict2, mm[0:L, 0:LANES], 0.0)
        m_ak = jnp.where(strict2, mm[0:L, LANES:2 * LANES], 0.0)
        p_rb = jnp.where(incl2, mm[L:2 * L, 0:LANES], 0.0)
        p_rk = jnp.where(incl2, mm[L:2 * L, LANES:2 * LANES], 0.0)

        t_bd = _neumann_inverse(stack_heads(m_ab))
        t_cat = (t_bd[0:L] + t_bd[L:2 * L]).astype(BF16)

        s = s_ref[j]
        s_b = s.astype(BF16)
        v_bd = stack_heads(v).astype(BF16)
        rhs = _dot(jnp.concatenate([a_t, m_ak], axis=1).astype(BF16), jnp.concatenate([s_b, v_bd], axis=0))
        u = _dot(t_cat, stack_heads(rhs).astype(BF16))
        u_bd = stack_heads(u).astype(BF16)
        o = _dot(jnp.concatenate([r_t, p_rb, p_rk], axis=1).astype(BF16),
                 jnp.concatenate([s_b, u_bd, v_bd], axis=0))
        upd = _dot_tn(jnp.concatenate([b * e_tail, k2 * e_tail], axis=0).astype(BF16),
                      jnp.concatenate([u, v], axis=0).astype(BF16))
        s_ref[j] = (jnp.exp(_colsum_bcast(lw, ones)) * s + upd) * bd_mask

        mean = _dot(o.astype(BF16), g128) * inv_hd
        dlt = o - mean
        var = _dot((dlt * dlt).astype(BF16), g128) * inv_hd
        o_n = dlt * lax.rsqrt(var + RWKV_GN_EPS) * lnw_ref[:, ls] + lnb_ref[:, ls]
        bonus = _dot((r * k2 * rk_ref[:, ls]).astype(BF16), g128) * v
        out_ref[:, ls] = ((o_n + bonus) * _silu(g)).astype(BF16)


def _rwkv(proj_main, proj_small, prm, v_first, batch, n_chunks, width):
    m = proj_main.shape[0]
    n_pairs = width // LANES
    has_vres = v_first is not None
    ns = proj_small.shape[1]
    row_blk = lambda w, cb=0: pl.BlockSpec((CHUNK, w), lambda b, c: (b * n_chunks + c, cb))
    vec = lambda a: pl.BlockSpec(a.shape, lambda b, c: (0, 0))
    args = [proj_main, proj_small, prm["mu_main"], prm["mu_small"], prm["w0"], prm["w2"], prm["a0"], prm["a2"],
            prm["k_k"], prm["k_a"], prm["r_k"], prm["lnx_w"], prm["lnx_b"]]
    in_specs = [row_blk(4 * width), row_blk(ns)] + [vec(a) for a in args[2:]]
    if has_vres:
        args += [v_first, prm["v0"], prm["v2"]]
        in_specs += [row_blk(width), vec(prm["v0"]), vec(prm["v2"])]
        out_specs = row_blk(width)
        out_shape = jax.ShapeDtypeStruct((m, width), BF16)
    else:
        out_specs = [row_blk(width), row_blk(width)]
        out_shape = [jax.ShapeDtypeStruct((m, width), BF16), jax.ShapeDtypeStruct((m, width), F32)]
    return pl.pallas_call(
        functools.partial(_rwkv_kernel, has_vres=has_vres, n_pairs=n_pairs, width=width),
        grid=(batch, n_chunks),
        in_specs=in_specs,
        out_specs=out_specs,
        out_shape=out_shape,
        scratch_shapes=[pltpu.VMEM((n_pairs, LANES, LANES), F32),
                        pltpu.VMEM((CARRY_ROWS, 4 * width), F32),
                        pltpu.VMEM((CARRY_ROWS, ns), F32)],
        compiler_params=pltpu.CompilerParams(dimension_semantics=("parallel", "arbitrary"),
                                             vmem_limit_bytes=VMEM_LIMIT_BYTES),
        name="rwkv7",
    )(*args)


def _gdn_kernel(qkv_ref, z_ref, sm_ref, cw_ref, al_ref, dtb_ref, nw_ref, out_ref,
                s_ref, carry_ref, act_ref, gate_ref, *, n_qk):
    L = CHUNK
    n_v = 2 * n_qk

    @pl.when(pl.program_id(1) == 0)
    def _():
        s_ref[...] = jnp.zeros_like(s_ref)
        carry_ref[...] = jnp.zeros_like(carry_ref)

    row1 = lax.broadcasted_iota(jnp.int32, (L, 1), 0)
    pad = jnp.zeros((L - CARRY_ROWS, LANES), F32)
    q_scale = GDN_HEAD_DIM ** -0.5

    for gi in range(4 * n_qk):
        sl = slice(gi * LANES, (gi + 1) * LANES)
        x = qkv_ref[:, sl]
        prev = carry_ref[:, sl]
        acc = x * cw_ref[GDN_CONV - 1:GDN_CONV, sl]
        for d in range(1, GDN_CONV):
            shifted = jnp.where(row1 < d, jnp.concatenate([pltpu.roll(prev, d, 0), pad], axis=0),
                                pltpu.roll(x, d, 0))
            acc = acc + shifted * cw_ref[GDN_CONV - 1 - d:GDN_CONV - d, sl]
        carry_ref[:, sl] = x[L - CARRY_ROWS:L, :]
        y = _silu(acc)
        if gi < 2 * n_qk:
            y = y * lax.rsqrt(jnp.sum(y * y, axis=-1, keepdims=True) + 1e-6)
            if gi < n_qk:
                y = y * q_scale
        act_ref[:, sl] = y

    row = lax.broadcasted_iota(jnp.int32, (L, L), 0)
    col = lax.broadcasted_iota(jnp.int32, (L, L), 1)
    tril = (row >= col).astype(BF16)
    sm = sm_ref[...]
    beta = _sigmoid(sm)
    g_log = -jnp.exp(al_ref[...]) * _softplus(sm + dtb_ref[...])
    g_cum = _cumsum_rows(tril, g_log)
    g_cum_t = jnp.concatenate([g_cum, jnp.zeros((LANES - L, LANES), F32)], axis=0).T
    for h in range(n_v):
        gate_ref[h, 0] = jnp.broadcast_to(g_cum[:, n_v + h:n_v + h + 1], (L, LANES))
        gate_ref[h, 1] = jnp.broadcast_to(beta[:, h:h + 1], (L, LANES))

    lane, m0, m1 = _lane_masks()
    low = lane < LANES // 2
    strict2, incl2 = _pair_time_masks()
    nw = nw_ref[...]

    for p in range(n_qk):
        q_h = act_ref[:, p * LANES:(p + 1) * LANES]
        k_h = act_ref[:, (n_qk + p) * LANES:(n_qk + p + 1) * LANES]
        q_b, k_b = q_h.astype(BF16), k_h.astype(BF16)
        mm = _dot_nt(jnp.concatenate([q_b, k_b], axis=0), jnp.concatenate([k_b, k_b], axis=0))
        heads = (2 * p, 2 * p + 1)
        g_full = [gate_ref[h, 0] for h in heads]
        b_full = [gate_ref[h, 1] for h in heads]
        g_cat = jnp.where(low, g_full[0], g_full[1])
        b_cat = jnp.where(low, b_full[0], b_full[1])
        g_row = (g_cum_t[n_v + heads[0]:n_v + heads[0] + 1, :]
                 + pltpu.roll(g_cum_t[n_v + heads[1]:n_v + heads[1] + 1, :], LANES // 2, 1))
        dm = jnp.where(incl2, jnp.exp(jnp.minimum(g_cat - g_row, 0.0)), 0.0)
        n_cat = jnp.where(strict2, -(b_cat * mm[L:2 * L] * dm), 0.0)
        t_bd = _neumann_inverse(jnp.concatenate([n_cat * m0, n_cat * m1], axis=0)).astype(BF16)
        qk_dm = mm[0:L] * dm

        rhs_rows, e_g = [], []
        for e, h in enumerate(heads):
            v_h = act_ref[:, (2 * n_qk + h) * LANES:(2 * n_qk + h + 1) * LANES]
            e_g.append(jnp.exp(g_full[e]))
            rhs_rows.append(jnp.concatenate([b_full[e] * v_h, (b_full[e] * e_g[e]) * k_h], axis=1))
        sol = _dot(t_bd, jnp.concatenate(rhs_rows, axis=0).astype(BF16))

        states = [s_ref[h] for h in heads]
        new_v = [sol[e * L:(e + 1) * L, 0:LANES]
                 - _dot(sol[e * L:(e + 1) * L, LANES:2 * LANES].astype(BF16), states[e].astype(BF16))
                 for e in range(2)]
        nv_stack = jnp.concatenate(new_v, axis=0).astype(BF16)
        for e, h in enumerate(heads):
            mask = m0 if e == 0 else m1
            lhs = jnp.concatenate([q_h * e_g[e], qk_dm * mask], axis=1).astype(BF16)
            o = _dot(lhs, jnp.concatenate([states[e].astype(BF16), nv_stack], axis=0))
            g_last = g_full[e][L - 1:L, :]
            k_tail = (k_h * jnp.exp(g_last - g_full[e])).astype(BF16)
            s_ref[h] = jnp.exp(g_last) * states[e] + _dot_tn(k_tail, new_v[e].astype(BF16))
            z = z_ref[:, h * LANES:(h + 1) * LANES]
            ms = jnp.mean(o * o, axis=-1, keepdims=True)
            out_ref[:, h * LANES:(h + 1) * LANES] = (o * lax.rsqrt(ms + NORM_EPS) * nw * _silu(z)).astype(BF16)


def _gdn(proj_main, proj_small, conv_w, a_log, dt_bias, norm_w, batch, n_chunks, n_qk):
    m = proj_main.shape[0]
    n_v = 2 * n_qk
    qkv_w = 4 * n_qk * LANES
    z_w = n_v * LANES
    row_blk = lambda w, cb=0: pl.BlockSpec((CHUNK, w), lambda b, c: (b * n_chunks + c, cb))
    vec = lambda a: pl.BlockSpec(a.shape, lambda b, c: (0, 0))
    cw = jnp.concatenate([conv_w, jnp.zeros((CARRY_ROWS - GDN_CONV, qkv_w), F32)], axis=0)
    al = jnp.zeros((1, LANES), F32).at[0, n_v:2 * n_v].set(a_log)
    dtb = jnp.zeros((1, LANES), F32).at[0, n_v:2 * n_v].set(dt_bias)
    nw = norm_w.reshape(1, LANES)
    return pl.pallas_call(
        functools.partial(_gdn_kernel, n_qk=n_qk),
        grid=(batch, n_chunks),
        in_specs=[row_blk(qkv_w), row_blk(z_w, qkv_w // z_w), row_blk(LANES), vec(cw), vec(al), vec(dtb), vec(nw)],
        out_specs=row_blk(z_w),
        out_shape=jax.ShapeDtypeStruct((m, z_w), BF16),
        scratch_shapes=[pltpu.VMEM((n_v, LANES, LANES), F32),
                        pltpu.VMEM((CARRY_ROWS, qkv_w), F32),
                        pltpu.VMEM((CHUNK, qkv_w), F32),
                        pltpu.VMEM((n_v, 2, CHUNK, LANES), F32)],
        compiler_params=pltpu.CompilerParams(dimension_semantics=("parallel", "arbitrary"),
                                             vmem_limit_bytes=VMEM_LIMIT_BYTES),
        name="gdn",
    )(proj_main, proj_main, proj_small, cw, al, dtb, nw)


def _pad_rows(w, rows):
    return jnp.concatenate([w, jnp.zeros((rows - w.shape[0], w.shape[1]), w.dtype)], axis=0)


def _even_params(j, width, ev_w_in, ev_mu, ev_w0, ev_w2, ev_a0, ev_a2, ev_k_k, ev_k_a, ev_r_k, ev_lnx_w, ev_lnx_b,
                 vr_w_down, vr_mu, vr_v0, vr_v2):
    d = ev_w_in.shape[1]
    w_in, mu = ev_w_in[j], ev_mu[j]
    rw = 4 * width
    lo_a, lo_b = rw + DECAY_RANK, rw + DECAY_RANK + ICL_RANK
    w_main = jnp.concatenate([w_in[:, :rw], w_in[:, lo_b:]], axis=1).astype(BF16)
    w_small = jnp.zeros((d, 3 * LANES), F32)
    w_small = w_small.at[:, 0:DECAY_RANK].set(w_in[:, rw:lo_a])
    w_small = w_small.at[:, LANES:LANES + ICL_RANK].set(w_in[:, lo_a:lo_b])
    mu_small = jnp.zeros((1, 3 * LANES), F32)
    mu_small = mu_small.at[0, 0:DECAY_RANK].set(mu[rw:lo_a])
    mu_small = mu_small.at[0, LANES:LANES + ICL_RANK].set(mu[lo_a:lo_b])
    row = lambda a: a.reshape(1, width)
    prm = dict(mu_main=mu[:rw].reshape(1, rw), w0=row(ev_w0[j]), w2=_pad_rows(ev_w2[j], LANES).astype(BF16),
               a0=row(ev_a0[j]), a2=_pad_rows(ev_a2[j], LANES).astype(BF16), k_k=row(ev_k_k[j]), k_a=row(ev_k_a[j]),
               r_k=row(ev_r_k[j]), lnx_w=row(ev_lnx_w[j]), lnx_b=row(ev_lnx_b[j]))
    if j > 0:
        w_small = w_small.at[:, 2 * LANES:2 * LANES + VRES_RANK].set(vr_w_down[j - 1])
        mu_small = mu_small.at[0, 2 * LANES:2 * LANES + VRES_RANK].set(vr_mu[j - 1])
        prm["v0"] = row(vr_v0[j - 1])
        prm["v2"] = _pad_rows(vr_v2[j - 1], LANES).astype(BF16)
    prm["mu_small"] = mu_small
    return w_main, w_small.astype(BF16), prm


def kernel(x, p, pre_g, post_g, ple_w_proj, ple_w_gate, ev_w_in, ev_mu, ev_w0, ev_w2, ev_a0, ev_a2, ev_k_k, ev_k_a, ev_r_k, ev_lnx_w, ev_lnx_b, vr_w_down, vr_mu, vr_v0, vr_v2, hg_lb, hg_norm_w, ev_w_out, od_w_in, od_conv_w, od_a_log, od_dt_bias, od_norm_w, od_w_out):
    batch, seq, d = x.shape
    depth = pre_g.shape[0]
    m = batch * seq
    n_chunks = seq // CHUNK
    width = ev_w0.shape[1]
    n_v = od_a_log.shape[1]
    n_qk = n_v // 2

    lb_all = jnp.cumsum(jax.nn.softmax(hg_lb.astype(F32), axis=0), axis=0)
    lb_all = lb_all - lb_all[0]

    xf = x.reshape(m, d)
    v_first = None
    for i in range(depth):
        j = i // 2
        if i % 2 == 0:
            w_main, w_small, prm = _even_params(j, width, ev_w_in, ev_mu, ev_w0, ev_w2, ev_a0, ev_a2, ev_k_k, ev_k_a,
                                                ev_r_k, ev_lnx_w, ev_lnx_b, vr_w_down, vr_mu, vr_v0, vr_v2)
            proj_main, proj_small = _norm_proj(xf, pre_g[i], w_main, w_small)
            if j == 0:
                mix_a, v_first = _rwkv(proj_main, proj_small, prm, None, batch, n_chunks, width)
            else:
                mix_a = _rwkv(proj_main, proj_small, prm, v_first, batch, n_chunks, width)
            mix_b = _hgrn(proj_main, 1, lb_all[j], hg_norm_w[j], j == 0, batch, n_chunks, width)
            w_out = ev_w_out[j].astype(BF16)
            mixes, w_outs = [mix_a, mix_b], [w_out[:width], w_out[width:]]
        else:
            w_in = od_w_in[j]
            n_main = (4 * n_qk + n_v) * LANES
            w_small = jnp.zeros((d, LANES), F32).at[:, 0:2 * n_v].set(w_in[:, n_main:])
            proj_main, proj_small = _norm_proj(xf, pre_g[i], w_in[:, :n_main].astype(BF16), w_small.astype(BF16))
            mix = _gdn(proj_main, proj_small, od_conv_w[j], od_a_log[j], od_dt_bias[j], od_norm_w[j],
                       batch, n_chunks, n_qk)
            mixes, w_outs = [mix], [od_w_out[j].astype(BF16)]
        xf = _post(mixes, w_outs, xf, post_g[i], p[i].reshape(m, -1), ple_w_gate[i].astype(BF16),
                   ple_w_proj[i].astype(BF16))
    return xf.reshape(batch, seq, d)
```

```python
import functools
import math

import jax
import jax.numpy as jnp
from jax import lax
from jax.experimental import pallas as pl
from jax.experimental.pallas import tpu as pltpu

F32 = jnp.float32
BF16 = jnp.bfloat16

LANES = 128
CARRY_ROWS = 8
VMEM_LIMIT_BYTES = 56 * 1024 * 1024

CHUNK = 64
SUB = 16
LOCKSTEP = 8
NORM_EPS = 1e-6
RWKV_HEAD_DIM = 64
RWKV_GN_EPS = 64e-5
DECAY_RANK = 64
ICL_RANK = 64
VRES_RANK = 32
GDN_HEAD_DIM = 128
GDN_CONV = 4


def _dot(a, b):
    return jnp.dot(a, b, preferred_element_type=F32)


def _dot_nt(a, b):
    return lax.dot_general(a, b, (((1,), (1,)), ((), ())), preferred_element_type=F32)


def _dot_tn(a, b):
    return lax.dot_general(a, b, (((0,), (0,)), ((), ())), preferred_element_type=F32)


def _split3(x):
    hi = x.astype(BF16)
    r1 = x - hi.astype(F32)
    mid = r1.astype(BF16)
    lo = (r1 - mid.astype(F32)).astype(BF16)
    return hi, mid, lo


def _cumsum_rows(tril, x):
    n = x.shape[1]
    cs = _dot(tril, jnp.concatenate(_split3(x), axis=1))
    return (cs[:, 0:n] + cs[:, n:2 * n]) + cs[:, 2 * n:3 * n]


def _sigmoid(x):
    return 1.0 / (1.0 + jnp.exp(-x))


def _silu(x):
    return x * _sigmoid(x)


def _softplus(x):
    return jnp.maximum(x, 0.0) + jnp.log(1.0 + jnp.exp(-jnp.abs(x)))


def _eye(size):
    return (lax.broadcasted_iota(jnp.int32, (size, size), 0)
            == lax.broadcasted_iota(jnp.int32, (size, size), 1)).astype(F32)


def _neumann_inverse_steps(n):
    nb = n.astype(BF16)
    p = _dot(nb, nb)
    yield None
    x = _eye(n.shape[0]) + n
    for _ in range(int(math.log2(CHUNK)) - 2):
        pb = p.astype(BF16)
        xp = _dot(x.astype(BF16), pb)
        p2 = _dot(pb, pb)
        yield None
        x = x + xp
        p = p2
    xp = _dot(x.astype(BF16), p.astype(BF16))
    yield None
    yield (x + xp,)


def _run_lockstep(make_chain, n_chains):
    for start in range(0, n_chains, LOCKSTEP):
        chains = [make_chain(i) for i in range(start, min(start + LOCKSTEP, n_chains))]
        while chains:
            alive = []
            for ch in chains:
                try:
                    next(ch)
                    alive.append(ch)
                except StopIteration:
                    pass
            chains = alive


def _lane_masks():
    lane = lax.broadcasted_iota(jnp.int32, (1, LANES), 1)
    m0 = (lane < LANES // 2).astype(F32)
    return lane, m0, 1.0 - m0


def _pair_time_masks():
    row = lax.broadcasted_iota(jnp.int32, (CHUNK, LANES), 0)
    src = lax.broadcasted_iota(jnp.int32, (CHUNK, LANES), 1) & (CHUNK - 1)
    return src < row, src <= row


def _tril_bf16():
    row = lax.broadcasted_iota(jnp.int32, (CHUNK, CHUNK), 0)
    col = lax.broadcasted_iota(jnp.int32, (CHUNK, CHUNK), 1)
    return (row >= col).astype(BF16)


def _norm_proj_kernel(x_ref, g_ref, wm_ref, ws_ref, om_ref, os_ref, h_ref):
    @pl.when(pl.program_id(1) == 0)
    def _():
        x = x_ref[...]
        ms = jnp.mean(x * x, axis=-1, keepdims=True)
        h = (x * lax.rsqrt(ms + NORM_EPS) * g_ref[...]).astype(BF16)
        h_ref[...] = h
        os_ref[...] = _dot(h, ws_ref[...])

    om_ref[...] = _dot(h_ref[...], wm_ref[...])


def _norm_proj(x, g, wm, ws):
    m, d = x.shape
    nm, ns = wm.shape[1], ws.shape[1]
    tm = min(1024, m)
    tn = min(1024, nm)
    return pl.pallas_call(
        _norm_proj_kernel,
        grid=(m // tm, nm // tn),
        in_specs=[pl.BlockSpec((tm, d), lambda i, j: (i, 0)),
                  pl.BlockSpec((1, d), lambda i, j: (0, 0)),
                  pl.BlockSpec((d, tn), lambda i, j: (0, j)),
                  pl.BlockSpec((d, ns), lambda i, j: (0, 0))],
        out_specs=[pl.BlockSpec((tm, tn), lambda i, j: (i, j)),
                   pl.BlockSpec((tm, ns), lambda i, j: (i, 0))],
        out_shape=[jax.ShapeDtypeStruct((m, nm), F32), jax.ShapeDtypeStruct((m, ns), F32)],
        scratch_shapes=[pltpu.VMEM((tm, d), BF16)],
        compiler_params=pltpu.CompilerParams(dimension_semantics=("parallel", "arbitrary"),
                                             vmem_limit_bytes=VMEM_LIMIT_BYTES),
        name="norm_proj",
    )(x, g.reshape(1, d), wm, ws)


def _post_kernel(*refs, n_mix):
    mix_refs, w_refs = refs[:n_mix], refs[n_mix:2 * n_mix]
    x_ref, pg_ref, p_ref, wg_ref, wp_ref, o_ref = refs[2 * n_mix:]
    y = _dot(mix_refs[0][...], w_refs[0][...])
    for mr, wr in zip(mix_refs[1:], w_refs[1:]):
        y = y + _dot(mr[...], wr[...])
    ms = jnp.mean(y * y, axis=-1, keepdims=True)
    x1 = x_ref[...] + y * lax.rsqrt(ms + NORM_EPS) * pg_ref[...]
    gate = _sigmoid(_dot(x1.astype(BF16), wg_ref[...]))
    o_ref[...] = x1 + gate * _dot(p_ref[...].astype(BF16), wp_ref[...])


def _post(mixes, w_outs, x, post_g, p, w_gate, w_proj):
    m, d = x.shape
    tm = min(256, m)
    n_mix = len(mixes)
    row_spec = lambda a: pl.BlockSpec((tm, a.shape[1]), lambda i: (i, 0))
    full_spec = lambda a: pl.BlockSpec(a.shape, lambda i: (0, 0))
    return pl.pallas_call(
        functools.partial(_post_kernel, n_mix=n_mix),
        grid=(m // tm,),
        in_specs=([row_spec(a) for a in mixes] + [full_spec(w) for w in w_outs]
                  + [row_spec(x), pl.BlockSpec((1, d), lambda i: (0, 0)), row_spec(p),
                     full_spec(w_gate), full_spec(w_proj)]),
        out_specs=row_spec(x),
        out_shape=jax.ShapeDtypeStruct((m, d), F32),
        compiler_params=pltpu.CompilerParams(dimension_semantics=("parallel",),
                                             vmem_limit_bytes=VMEM_LIMIT_BYTES),
        name="post",
    )(*mixes, *w_outs, x, post_g.reshape(1, d), p, w_gate, w_proj)


def _hgrn_kernel(pm_ref, lb_ref, nw_ref, sel_ref, out_ref, s_ref, kc_ref, *, first, n_heads, width):
    L = CHUNK

    @pl.when(pl.program_id(1) == 0)
    def _():
        s_ref[...] = jnp.zeros_like(s_ref)

    row = lax.broadcasted_iota(jnp.int32, (L, L), 0)
    col = lax.broadcasted_iota(jnp.int32, (L, L), 1)
    tril = _tril_bf16()
    row_blk, col_blk = row // SUB, col // SUB
    off_mask = col_blk < row_blk
    diag_mask = jnp.logical_and(col_blk == row_blk, col <= row)
    nw = nw_ref[...]
    sel = sel_ref[...]

    def head_chain(h):
        ls = slice(h * LANES, (h + 1) * LANES)
        qp = pm_ref[:, h * LANES:(h + 1) * LANES]
        z = pm_ref[:, width + h * LANES:width + (h + 1) * LANES]
        v = pm_ref[:, 2 * width + h * LANES:2 * width + (h + 1) * LANES].astype(BF16)

        e = jnp.exp(-jnp.abs(z))
        inv = 1.0 / (1.0 + e)
        pos = z >= 0.0
        sig = jnp.where(pos, inv, e * inv)
        nsig = jnp.where(pos, e * inv, inv)
        if first:
            lf = jnp.minimum(z, 0.0) - jnp.log(1.0 + e)
            k = nsig
        else:
            lb = lb_ref[:, ls]
            lf = jnp.log(lb + (1.0 - lb) * sig)
            k = (1.0 - lb) * nsig
        q = _silu(qp)
        c = _cumsum_rows(tril, lf)
        yield
        kc_ref[h, 0] = k
        kc_ref[h, 1] = c

        att_rows = [jnp.zeros((SUB, L), F32)]
        for i in range(1, L // SUB):
            ref_c = kc_ref[h, 1, i * SUB:i * SUB + 1, :]
            kt = (k * jnp.exp(jnp.minimum(ref_c - c, 0.0))).astype(BF16)
            qt = (q[i * SUB:(i + 1) * SUB] * jnp.exp(c[i * SUB:(i + 1) * SUB] - ref_c)).astype(BF16)
            att_rows.append(_dot_nt(qt, kt))

        p_rows = []
        for i in range(L // SUB):
            q_i = q[i * SUB:(i + 1) * SUB]
            c_i = c[i * SUB:(i + 1) * SUB]
            pieces = []
            for j in range(SUB):
                k_row = kc_ref[h, 0, i * SUB + j:i * SUB + j + 1, :]
                c_row = kc_ref[h, 1, i * SUB + j:i * SUB + j + 1, :]
                dec = jnp.exp(jnp.minimum(c_i - c_row, 0.0))
                pieces.append((q_i * dec * k_row).astype(BF16))
            p_rows.append(jnp.concatenate(pieces, axis=1))
        att_diag = _dot(jnp.concatenate(p_rows, axis=0), sel)

        st = s_ref[h]
        qs = _dot_nt((q * jnp.exp(c)).astype(BF16), st.astype(BF16))
        c_last = kc_ref[h, 1, L - 1:L, :]
        upd = _dot_tn(v, (k * jnp.exp(c_last - c)).astype(BF16))
        yield
        att = (jnp.where(off_mask, jnp.concatenate(att_rows, axis=0), 0.0)
               + jnp.where(diag_mask, att_diag[:, :L], 0.0))
        o = qs + _dot(att.astype(BF16), v)
        s_ref[h] = jnp.exp(c_last) * st + upd
        yield
        gb = pm_ref[:, 3 * width + h * LANES:3 * width + (h + 1) * LANES]
        ms = jnp.mean(o * o, axis=-1, keepdims=True)
        out_ref[:, ls] = (o * lax.rsqrt(ms + NORM_EPS) * nw * _silu(gb)).astype(BF16)

    _run_lockstep(head_chain, n_heads)


def _hgrn(proj_main, col_block, lb, norm_w, first, batch, n_chunks, width):
    m = proj_main.shape[0]
    n_heads = width // LANES
    j = jnp.arange(SUB * LANES) // LANES
    s = jnp.arange(LANES)
    sel = ((s[None, :] % SUB == j[:, None]) & (s[None, :] < CHUNK)).astype(BF16)
    return pl.pallas_call(
        functools.partial(_hgrn_kernel, first=first, n_heads=n_heads, width=width),
        grid=(batch, n_chunks),
        in_specs=[pl.BlockSpec((CHUNK, 4 * width), lambda b, c: (b * n_chunks + c, col_block)),
                  pl.BlockSpec((1, width), lambda b, c: (0, 0)),
                  pl.BlockSpec((1, LANES), lambda b, c: (0, 0)),
                  pl.BlockSpec((SUB * LANES, LANES), lambda b, c: (0, 0))],
        out_specs=pl.BlockSpec((CHUNK, width), lambda b, c: (b * n_chunks + c, 0)),
        out_shape=jax.ShapeDtypeStruct((m, width), BF16),
        scratch_shapes=[pltpu.VMEM((n_heads, LANES, LANES), F32),
                        pltpu.VMEM((n_heads, 2, CHUNK, LANES), F32)],
        compiler_params=pltpu.CompilerParams(dimension_semantics=("parallel", "arbitrary"),
                                             vmem_limit_bytes=VMEM_LIMIT_BYTES),
        name="hgrn2",
    )(proj_main, lb.reshape(1, width), norm_w.reshape(1, LANES), sel)


def _rwkv_kernel(*refs, has_vres, n_pairs, width):
    (pm_ref, ps_ref, mum_ref, mus_ref, w0_ref, w2_ref, a0_ref, a2_ref, kk_ref, ka_ref, rk_ref,
     lnw_ref, lnb_ref) = refs[:13]
    if has_vres:
        vf_ref, v0_ref, v2_ref, out_ref, s_ref, cm_ref, cs_ref, fm_ref = refs[13:]
        vfo_ref = None
    else:
        out_ref, vfo_ref, s_ref, cm_ref, cs_ref, fm_ref = refs[13:]
    L = CHUNK

    @pl.when(pl.program_id(1) == 0)
    def _():
        s_ref[...] = jnp.zeros_like(s_ref)
        cm_ref[...] = jnp.zeros_like(cm_ref)
        cs_ref[...] = jnp.zeros_like(cs_ref)

    row1 = lax.broadcasted_iota(jnp.int32, (L, 1), 0)

    def token_shift(x, carry_row, mu):
        prev = jnp.where(row1 == 0, carry_row, pltpu.roll(x, 1, 0))
        return x + (prev - x) * mu

    tril = _tril_bf16()

    ps = ps_ref[...]
    pss = token_shift(ps, cs_ref[0:1, :], mus_ref[...])
    cs_ref[0:1, :] = ps[L - 1:L, :]
    wd_t = jnp.tanh(pss[:, 0:LANES]).astype(BF16)
    ad = pss[:, LANES:2 * LANES].astype(BF16)
    lw_all = -math.exp(-0.5) * _sigmoid(w0_ref[...] + _dot(wd_t, w2_ref[...]))
    fm_ref[0] = lw_all
    fm_ref[1] = _cumsum_rows(tril, lw_all)
    fm_ref[2] = _sigmoid(a0_ref[...] + _dot(ad, a2_ref[...]))
    if has_vres:
        vd = pss[:, 2 * LANES:3 * LANES].astype(BF16)
        fm_ref[3] = _sigmoid(v0_ref[...] + _dot(vd, v2_ref[...]))

    _, m0, m1 = _lane_masks()
    strict2, incl2 = _pair_time_masks()
    rr = lax.broadcasted_iota(jnp.int32, (LANES, LANES), 0) // RWKV_HEAD_DIM
    cc = lax.broadcasted_iota(jnp.int32, (LANES, LANES), 1) // RWKV_HEAD_DIM
    bd_mask = (rr == cc).astype(F32)
    g128 = bd_mask.astype(BF16)
    inv_hd = 1.0 / RWKV_HEAD_DIM

    def stack_heads(x):
        return jnp.concatenate([x * m0, x * m1], axis=0)

    def pair_chain(j):
        ls = slice(j * LANES, (j + 1) * LANES)

        def load(q):
            sl = slice(q * width + j * LANES, q * width + (j + 1) * LANES)
            x = pm_ref[:, sl]
            y = token_shift(x, cm_ref[0:1, sl], mum_ref[:, sl])
            cm_ref[0:1, sl] = x[L - 1:L, :]
            return y

        r, k, v = load(0), load(1), load(2)
        lw, c, icl = fm_ref[0, :, ls], fm_ref[1, :, ls], fm_ref[2, :, ls]
        if has_vres:
            v = v + (vf_ref[:, ls] - v) * fm_ref[3, :, ls]
        else:
            vfo_ref[:, ls] = v
        kkr = k * kk_ref[:, ls]
        k2 = k * (1.0 + (icl - 1.0) * ka_ref[:, ls])
        sums = _dot(jnp.concatenate([kkr * kkr, r * k2 * rk_ref[:, ls]], axis=0).astype(BF16), g128)
        yield
        kkn = kkr * lax.rsqrt(sums[0:L] + 1e-12)
        bonus = sums[L:2 * L] * v
        b = kkn * icl
        c_last = c[L - 1:L, :]
        e_neg = jnp.exp(-c)
        e_tail = jnp.exp(c_last - c)
        a_t = -kkn * jnp.exp(c - lw)
        r_t = r * jnp.exp(c)
        lhs = jnp.concatenate([a_t, r_t], axis=0).astype(BF16)
        rhs_t = jnp.concatenate([stack_heads(b * e_neg), stack_heads(k2 * e_neg)], axis=0).astype(BF16)
        mm = _dot_nt(lhs, rhs_t)
        st = s_ref[j]
        sr = _dot_nt(lhs, st.astype(BF16))
        yield
        m_ab = jnp.where(strict2, mm[0:L, 0:LANES], 0.0)
        m_ak = jnp.where(strict2, mm[0:L, LANES:2 * LANES], 0.0)
        p_rb = jnp.where(incl2, mm[L:2 * L, 0:LANES], 0.0)
        p_rk = jnp.where(incl2, mm[L:2 * L, LANES:2 * LANES], 0.0)
        v_bd = stack_heads(v).astype(BF16)
        rhs = sr[0:L] + _dot(m_ak.astype(BF16), v_bd)
        inverse = _neumann_inverse_steps(stack_heads(m_ab))
        for step in inverse:
            if step is None:
                yield
            else:
                t_bd, = step
        t_cat = (t_bd[0:L] + t_bd[L:2 * L]).astype(BF16)
        u = _dot(t_cat, stack_heads(rhs).astype(BF16))
        yield
        u_bd = stack_heads(u).astype(BF16)
        o = sr[L:2 * L] + _dot(jnp.concatenate([p_rb, p_rk], axis=1).astype(BF16),
                               jnp.concatenate([u_bd, v_bd], axis=0))
        upd = _dot_tn(jnp.concatenate([u, v], axis=0).astype(BF16),
                      jnp.concatenate([b * e_tail, k2 * e_tail], axis=0).astype(BF16))
        yield
        s_ref[j] = (jnp.exp(c_last) * st + upd) * bd_mask
        mean = _dot(o.astype(BF16), g128) * inv_hd
        yield
        dlt = o - mean
        var = _dot((dlt * dlt).astype(BF16), g128) * inv_hd
        yield
        g = load(3)
        o_n = dlt * lax.rsqrt(var + RWKV_GN_EPS) * lnw_ref[:, ls] + lnb_ref[:, ls]
        out_ref[:, ls] = ((o_n + bonus) * _silu(g)).astype(BF16)

    _run_lockstep(pair_chain, n_pairs)


def _rwkv(proj_main, proj_small, prm, v_first, batch, n_chunks, width):
    m = proj_main.shape[0]
    n_pairs = width // LANES
    has_vres = v_first is not None
    ns = proj_small.shape[1]
    row_blk = lambda w, cb=0: pl.BlockSpec((CHUNK, w), lambda b, c: (b * n_chunks + c, cb))
    vec = lambda a: pl.BlockSpec(a.shape, lambda b, c: (0, 0))
    args = [proj_main, proj_small, prm["mu_main"], prm["mu_small"], prm["w0"], prm["w2"], prm["a0"], prm["a2"],
            prm["k_k"], prm["k_a"], prm["r_k"], prm["lnx_w"], prm["lnx_b"]]
    in_specs = [row_blk(4 * width), row_blk(ns)] + [vec(a) for a in args[2:]]
    if has_vres:
        args += [v_first, prm["v0"], prm["v2"]]
        in_specs += [row_blk(width), vec(prm["v0"]), vec(prm["v2"])]
        out_specs = row_blk(width)
        out_shape = jax.ShapeDtypeStruct((m, width), BF16)
    else:
        out_specs = [row_blk(width), row_blk(width)]
        out_shape = [jax.ShapeDtypeStruct((m, width), BF16), jax.ShapeDtypeStruct((m, width), F32)]
    return pl.pallas_call(
        functools.partial(_rwkv_kernel, has_vres=has_vres, n_pairs=n_pairs, width=width),
        grid=(batch, n_chunks),
        in_specs=in_specs,
        out_specs=out_specs,
        out_shape=out_shape,
        scratch_shapes=[pltpu.VMEM((n_pairs, LANES, LANES), F32),
                        pltpu.VMEM((CARRY_ROWS, 4 * width), F32),
                        pltpu.VMEM((CARRY_ROWS, ns), F32),
                        pltpu.VMEM((4, CHUNK, width), F32)],
        compiler_params=pltpu.CompilerParams(dimension_semantics=("parallel", "arbitrary"),
                                             vmem_limit_bytes=VMEM_LIMIT_BYTES),
        name="rwkv7",
    )(*args)


def _gdn_kernel(qkv_ref, z_ref, sm_ref, cw_ref, al_ref, dtb_ref, nw_ref, out_ref,
                s_ref, carry_ref, act_ref, gate_ref, *, n_qk):
    L = CHUNK
    n_v = 2 * n_qk

    @pl.when(pl.program_id(1) == 0)
    def _():
        s_ref[...] = jnp.zeros_like(s_ref)
        carry_ref[...] = jnp.zeros_like(carry_ref)

    row1 = lax.broadcasted_iota(jnp.int32, (L, 1), 0)
    pad = jnp.zeros((L - CARRY_ROWS, LANES), F32)
    q_scale = GDN_HEAD_DIM ** -0.5

    for gi in range(4 * n_qk):
        sl = slice(gi * LANES, (gi + 1) * LANES)
        x = qkv_ref[:, sl]
        prev = carry_ref[:, sl]
        acc = x * cw_ref[GDN_CONV - 1:GDN_CONV, sl]
        for d in range(1, GDN_CONV):
            shifted = jnp.where(row1 < d, jnp.concatenate([pltpu.roll(prev, d, 0), pad], axis=0),
                                pltpu.roll(x, d, 0))
            acc = acc + shifted * cw_ref[GDN_CONV - 1 - d:GDN_CONV - d, sl]
        carry_ref[:, sl] = x[L - CARRY_ROWS:L, :]
        y = _silu(acc)
        if gi < 2 * n_qk:
            y = y * lax.rsqrt(jnp.sum(y * y, axis=-1, keepdims=True) + 1e-6)
            if gi < n_qk:
                y = y * q_scale
        act_ref[:, sl] = y

    sm = sm_ref[...]
    beta = _sigmoid(sm)
    g_log = -jnp.exp(al_ref[...]) * _softplus(sm + dtb_ref[...])
    g_cum = _cumsum_rows(_tril_bf16(), g_log)
    g_cum_t = jnp.concatenate([g_cum, jnp.zeros((LANES - L, LANES), F32)], axis=0).T
    for h in range(n_v):
        gate_ref[h, 0] = jnp.broadcast_to(g_cum[:, n_v + h:n_v + h + 1], (L, LANES))
        gate_ref[h, 1] = jnp.broadcast_to(beta[:, h:h + 1], (L, LANES))

    lane, m0, m1 = _lane_masks()
    low = lane < LANES // 2
    strict2, incl2 = _pair_time_masks()
    nw = nw_ref[...]

    def pair_chain(p):
        q_h = act_ref[:, p * LANES:(p + 1) * LANES]
        k_h = act_ref[:, (n_qk + p) * LANES:(n_qk + p + 1) * LANES]
        q_b, k_b = q_h.astype(BF16), k_h.astype(BF16)
        mm = _dot_nt(jnp.concatenate([q_b, k_b], axis=0), jnp.concatenate([k_b, k_b], axis=0))
        yield
        heads = (2 * p, 2 * p + 1)
        g_full = [gate_ref[h, 0] for h in heads]
        b_full = [gate_ref[h, 1] for h in heads]
        g_cat = jnp.where(low, g_full[0], g_full[1])
        b_cat = jnp.where(low, b_full[0], b_full[1])
        g_row = (g_cum_t[n_v + heads[0]:n_v + heads[0] + 1, :]
                 + pltpu.roll(g_cum_t[n_v + heads[1]:n_v + heads[1] + 1, :], LANES // 2, 1))
        dm = jnp.where(incl2, jnp.exp(jnp.minimum(g_cat - g_row, 0.0)), 0.0)
        n_cat = jnp.where(strict2, -(b_cat * mm[L:2 * L] * dm), 0.0)
        qk_dm = mm[0:L] * dm
        inverse = _neumann_inverse_steps(jnp.concatenate([n_cat * m0, n_cat * m1], axis=0))
        for step in inverse:
            if step is None:
                yield
            else:
                t_bd, = step

        rhs_rows, e_g = [], []
        for e, h in enumerate(heads):
            v_h = act_ref[:, (2 * n_qk + h) * LANES:(2 * n_qk + h + 1) * LANES]
            e_g.append(jnp.exp(g_full[e]))
            rhs_rows.append(jnp.concatenate([b_full[e] * v_h, (b_full[e] * e_g[e]) * k_h], axis=1))
        sol = _dot(t_bd.astype(BF16), jnp.concatenate(rhs_rows, axis=0).astype(BF16))
        yield
        states = [s_ref[h] for h in heads]
        states_b = [s.astype(BF16) for s in states]
        sks = [_dot(sol[e * L:(e + 1) * L, LANES:2 * LANES].astype(BF16), states_b[e]) for e in range(2)]
        yield
        new_v = [sol[e * L:(e + 1) * L, 0:LANES] - sks[e] for e in range(2)]
        nv_stack = jnp.concatenate(new_v, axis=0).astype(BF16)
        outs, upds, g_lasts = [], [], []
        for e, h in enumerate(heads):
            mask = m0 if e == 0 else m1
            lhs = jnp.concatenate([q_h * e_g[e], qk_dm * mask], axis=1).astype(BF16)
            outs.append(_dot(lhs, jnp.concatenate([states_b[e], nv_stack], axis=0)))
            g_lasts.append(g_full[e][L - 1:L, :])
            k_tail = (k_h * jnp.exp(g_lasts[e] - g_full[e])).astype(BF16)
            upds.append(_dot_tn(k_tail, nv_stack[e * L:(e + 1) * L]))
        yield
        for e, h in enumerate(heads):
            s_ref[h] = jnp.exp(g_lasts[e]) * states[e] + upds[e]
            o = outs[e]
            z = z_ref[:, h * LANES:(h + 1) * LANES]
            ms = jnp.mean(o * o, axis=-1, keepdims=True)
            out_ref[:, h * LANES:(h + 1) * LANES] = (o * lax.rsqrt(ms + NORM_EPS) * nw * _silu(z)).astype(BF16)

    _run_lockstep(pair_chain, n_qk)


def _gdn(proj_main, proj_small, conv_w, a_log, dt_bias, norm_w, batch, n_chunks, n_qk):
    m = proj_main.shape[0]
    n_v = 2 * n_qk
    qkv_w = 4 * n_qk * LANES
    z_w = n_v * LANES
    row_blk = lambda w, cb=0: pl.BlockSpec((CHUNK, w), lambda b, c: (b * n_chunks + c, cb))
    vec = lambda a: pl.BlockSpec(a.shape, lambda b, c: (0, 0))
    cw = jnp.concatenate([conv_w, jnp.zeros((CARRY_ROWS - GDN_CONV, qkv_w), F32)], axis=0)
    al = jnp.zeros((1, LANES), F32).at[0, n_v:2 * n_v].set(a_log)
    dtb = jnp.zeros((1, LANES), F32).at[0, n_v:2 * n_v].set(dt_bias)
    nw = norm_w.reshape(1, LANES)
    return pl.pallas_call(
        functools.partial(_gdn_kernel, n_qk=n_qk),
        grid=(batch, n_chunks),
        in_specs=[row_blk(qkv_w), row_blk(z_w, qkv_w // z_w), row_blk(LANES), vec(cw), vec(al), vec(dtb), vec(nw)],
        out_specs=row_blk(z_w),
        out_shape=jax.ShapeDtypeStruct((m, z_w), BF16),
        scratch_shapes=[pltpu.VMEM((n_v, LANES, LANES), F32),
                        pltpu.VMEM((CARRY_ROWS, qkv_w), F32),
                        pltpu.VMEM((CHUNK, qkv_w), F32),
                        pltpu.VMEM((n_v, 2, CHUNK, LANES), F32)],
        compiler_params=pltpu.CompilerParams(dimension_semantics=("parallel", "arbitrary"),
                                             vmem_limit_bytes=VMEM_LIMIT_BYTES),
        name="gdn",
    )(proj_main, proj_main, proj_small, cw, al, dtb, nw)


def _pad_rows(w, rows):
    return jnp.concatenate([w, jnp.zeros((rows - w.shape[0], w.shape[1]), w.dtype)], axis=0)


def _even_params(j, width, ev_w_in, ev_mu, ev_w0, ev_w2, ev_a0, ev_a2, ev_k_k, ev_k_a, ev_r_k, ev_lnx_w, ev_lnx_b,
                 vr_w_down, vr_mu, vr_v0, vr_v2):
    d = ev_w_in.shape[1]
    w_in, mu = ev_w_in[j], ev_mu[j]
    rw = 4 * width
    lo_a, lo_b = rw + DECAY_RANK, rw + DECAY_RANK + ICL_RANK
    w_main = jnp.concatenate([w_in[:, :rw], w_in[:, lo_b:]], axis=1).astype(BF16)
    w_small = jnp.zeros((d, 3 * LANES), F32)
    w_small = w_small.at[:, 0:DECAY_RANK].set(w_in[:, rw:lo_a])
    w_small = w_small.at[:, LANES:LANES + ICL_RANK].set(w_in[:, lo_a:lo_b])
    mu_small = jnp.zeros((1, 3 * LANES), F32)
    mu_small = mu_small.at[0, 0:DECAY_RANK].set(mu[rw:lo_a])
    mu_small = mu_small.at[0, LANES:LANES + ICL_RANK].set(mu[lo_a:lo_b])
    row = lambda a: a.reshape(1, width)
    prm = dict(mu_main=mu[:rw].reshape(1, rw), w0=row(ev_w0[j]), w2=_pad_rows(ev_w2[j], LANES).astype(BF16),
               a0=row(ev_a0[j]), a2=_pad_rows(ev_a2[j], LANES).astype(BF16), k_k=row(ev_k_k[j]), k_a=row(ev_k_a[j]),
               r_k=row(ev_r_k[j]), lnx_w=row(ev_lnx_w[j]), lnx_b=row(ev_lnx_b[j]))
    if j > 0:
        w_small = w_small.at[:, 2 * LANES:2 * LANES + VRES_RANK].set(vr_w_down[j - 1])
        mu_small = mu_small.at[0, 2 * LANES:2 * LANES + VRES_RANK].set(vr_mu[j - 1])
        prm["v0"] = row(vr_v0[j - 1])
        prm["v2"] = _pad_rows(vr_v2[j - 1], LANES).astype(BF16)
    prm["mu_small"] = mu_small
    return w_main, w_small.astype(BF16), prm


def kernel(x, p, pre_g, post_g, ple_w_proj, ple_w_gate, ev_w_in, ev_mu, ev_w0, ev_w2, ev_a0, ev_a2, ev_k_k, ev_k_a, ev_r_k, ev_lnx_w, ev_lnx_b, vr_w_down, vr_mu, vr_v0, vr_v2, hg_lb, hg_norm_w, ev_w_out, od_w_in, od_conv_w, od_a_log, od_dt_bias, od_norm_w, od_w_out):
    batch, seq, d = x.shape
    depth = pre_g.shape[0]
    m = batch * seq
    n_chunks = seq // CHUNK
    width = ev_w0.shape[1]
    n_v = od_a_log.shape[1]
    n_qk = n_v // 2

    lb_all = jnp.cumsum(jax.nn.softmax(hg_lb.astype(F32), axis=0), axis=0)
    lb_all = lb_all - lb_all[0]

    xf = x.reshape(m, d)
    v_first = None
    for i in range(depth):
        j = i // 2
        if i % 2 == 0:
            w_main, w_small, prm = _even_params(j, width, ev_w_in, ev_mu, ev_w0, ev_w2, ev_a0, ev_a2, ev_k_k, ev_k_a,
                                                ev_r_k, ev_lnx_w, ev_lnx_b, vr_w_down, vr_mu, vr_v0, vr_v2)
            proj_main, proj_small = _norm_proj(xf, pre_g[i], w_main, w_small)
            if j == 0:
                mix_a, v_first = _rwkv(proj_main, proj_small, prm, None, batch, n_chunks, width)
            else:
                mix_a = _rwkv(proj_main, proj_small, prm, v_first, batch, n_chunks, width)
            mix_b = _hgrn(proj_main, 1, lb_all[j], hg_norm_w[j], j == 0, batch, n_chunks, width)
            w_out = ev_w_out[j].astype(BF16)
            mixes, w_outs = [mix_a, mix_b], [w_out[:width], w_out[width:]]
        else:
            w_in = od_w_in[j]
            n_main = (4 * n_qk + n_v) * LANES
            w_small = jnp.zeros((d, LANES), F32).at[:, 0:2 * n_v].set(w_in[:, n_main:])
            proj_main, proj_small = _norm_proj(xf, pre_g[i], w_in[:, :n_main].astype(BF16), w_small.astype(BF16))
            mix = _gdn(proj_main, proj_small, od_conv_w[j], od_a_log[j], od_dt_bias[j], od_norm_w[j],
                       batch, n_chunks, n_qk)
            mixes, w_outs = [mix], [od_w_out[j].astype(BF16)]
        xf = _post(mixes, w_outs, xf, post_g[i], p[i].reshape(m, -1), ple_w_gate[i].astype(BF16),
                   ple_w_proj[i].astype(BF16))
    return xf.reshape(batch, seq, d)
```

```python
import functools
import math

import jax
import jax.numpy as jnp
from jax import lax
from jax.experimental import pallas as pl
from jax.experimental.pallas import tpu as pltpu

F32 = jnp.float32
BF16 = jnp.bfloat16

LANES = 128
CARRY_ROWS = 8
VMEM_LIMIT_BYTES = 56 * 1024 * 1024

CHUNK = 64
SUB = 16
SEQS_PER_STEP = 2
HGRN_LOCKSTEP = 16
RWKV_LOCKSTEP = 16
GDN_LOCKSTEP = 8
NORM_EPS = 1e-6
RWKV_HEAD_DIM = 64
RWKV_GN_EPS = 64e-5
DECAY_RANK = 64
ICL_RANK = 64
VRES_RANK = 32
GDN_HEAD_DIM = 128
GDN_CONV = 4
LOG2E = math.log2(math.e)


def _dot(a, b):
    return jnp.dot(a, b, preferred_element_type=F32)


def _dot_nt(a, b):
    return lax.dot_general(a, b, (((1,), (1,)), ((), ())), preferred_element_type=F32)


def _dot_tn(a, b):
    return lax.dot_general(a, b, (((0,), (0,)), ((), ())), preferred_element_type=F32)


def _split3(x):
    hi = x.astype(BF16)
    r1 = x - hi.astype(F32)
    mid = r1.astype(BF16)
    lo = (r1 - mid.astype(F32)).astype(BF16)
    return hi, mid, lo


def _cumsum_rows(tril, x):
    n = x.shape[1]
    cs = _dot(tril, jnp.concatenate(_split3(x), axis=1))
    return (cs[:, 0:n] + cs[:, n:2 * n]) + cs[:, 2 * n:3 * n]


def _sigmoid(x):
    return 1.0 / (1.0 + jnp.exp(-x))


def _silu(x):
    return x * _sigmoid(x)


def _softplus(x):
    return jnp.maximum(x, 0.0) + jnp.log(1.0 + jnp.exp(-jnp.abs(x)))


def _eye(size):
    return (lax.broadcasted_iota(jnp.int32, (size, size), 0)
            == lax.broadcasted_iota(jnp.int32, (size, size), 1)).astype(F32)


def _neumann_inverse_steps(n):
    nb = n.astype(BF16)
    p = _dot(nb, nb)
    yield None
    x = _eye(n.shape[0]) + n
    for _ in range(int(math.log2(CHUNK)) - 2):
        pb = p.astype(BF16)
        xp = _dot(x.astype(BF16), pb)
        p2 = _dot(pb, pb)
        yield None
        x = x + xp
        p = p2
    xp = _dot(x.astype(BF16), p.astype(BF16))
    yield None
    yield (x + xp,)


def _run_lockstep(chain_makers, group):
    chain_makers = list(chain_makers)
    for start in range(0, len(chain_makers), group):
        chains = [make() for make in chain_makers[start:start + group]]
        while chains:
            alive = []
            for ch in chains:
                try:
                    next(ch)
                    alive.append(ch)
                except StopIteration:
                    pass
            chains = alive


def _lane_masks():
    lane = lax.broadcasted_iota(jnp.int32, (1, LANES), 1)
    m0 = (lane < LANES // 2).astype(F32)
    return lane, m0, 1.0 - m0


def _pair_time_masks():
    row = lax.broadcasted_iota(jnp.int32, (CHUNK, LANES), 0)
    src = lax.broadcasted_iota(jnp.int32, (CHUNK, LANES), 1) & (CHUNK - 1)
    return src < row, src <= row


def _tril_bf16():
    row = lax.broadcasted_iota(jnp.int32, (CHUNK, CHUNK), 0)
    col = lax.broadcasted_iota(jnp.int32, (CHUNK, CHUNK), 1)
    return (row >= col).astype(BF16)


def _seq_block(width, n_chunks, col_block=0):
    return pl.BlockSpec((SEQS_PER_STEP, CHUNK, width), lambda b, c: (b, c, col_block))


def _const_block(a):
    return pl.BlockSpec(a.shape, lambda b, c: (0,) * a.ndim)


def _norm_proj_kernel(x_ref, g_ref, wm_ref, ws_ref, om_ref, os_ref, h_ref):
    @pl.when(pl.program_id(1) == 0)
    def _():
        x = x_ref[...]
        ms = jnp.mean(x * x, axis=-1, keepdims=True)
        h = (x * lax.rsqrt(ms + NORM_EPS) * g_ref[...]).astype(BF16)
        h_ref[...] = h
        os_ref[...] = _dot(h, ws_ref[...])

    om_ref[...] = _dot(h_ref[...], wm_ref[...])


def _norm_proj(x, g, wm, ws):
    m, d = x.shape
    nm, ns = wm.shape[1], ws.shape[1]
    tm = min(1024, m)
    tn = min(1024, nm)
    return pl.pallas_call(
        _norm_proj_kernel,
        grid=(m // tm, nm // tn),
        in_specs=[pl.BlockSpec((tm, d), lambda i, j: (i, 0)),
                  pl.BlockSpec((1, d), lambda i, j: (0, 0)),
                  pl.BlockSpec((d, tn), lambda i, j: (0, j)),
                  pl.BlockSpec((d, ns), lambda i, j: (0, 0))],
        out_specs=[pl.BlockSpec((tm, tn), lambda i, j: (i, j)),
                   pl.BlockSpec((tm, ns), lambda i, j: (i, 0))],
        out_shape=[jax.ShapeDtypeStruct((m, nm), F32), jax.ShapeDtypeStruct((m, ns), F32)],
        scratch_shapes=[pltpu.VMEM((tm, d), BF16)],
        compiler_params=pltpu.CompilerParams(dimension_semantics=("parallel", "arbitrary"),
                                             vmem_limit_bytes=VMEM_LIMIT_BYTES),
        name="norm_proj",
    )(x, g.reshape(1, d), wm, ws)


def _post_kernel(*refs, n_mix):
    mix_refs, w_refs = refs[:n_mix], refs[n_mix:2 * n_mix]
    x_ref, pg_ref, p_ref, wg_ref, wp_ref, o_ref = refs[2 * n_mix:]
    y = _dot(mix_refs[0][...], w_refs[0][...])
    for mr, wr in zip(mix_refs[1:], w_refs[1:]):
        y = y + _dot(mr[...], wr[...])
    ms = jnp.mean(y * y, axis=-1, keepdims=True)
    x1 = x_ref[...] + y * lax.rsqrt(ms + NORM_EPS) * pg_ref[...]
    gate = _sigmoid(_dot(x1.astype(BF16), wg_ref[...]))
    o_ref[...] = x1 + gate * _dot(p_ref[...].astype(BF16), wp_ref[...])


def _post(mixes, w_outs, x, post_g, p, w_gate, w_proj):
    m, d = x.shape
    tm = min(256, m)
    n_mix = len(mixes)
    row_spec = lambda a: pl.BlockSpec((tm, a.shape[1]), lambda i: (i, 0))
    full_spec = lambda a: pl.BlockSpec(a.shape, lambda i: (0, 0))
    return pl.pallas_call(
        functools.partial(_post_kernel, n_mix=n_mix),
        grid=(m // tm,),
        in_specs=([row_spec(a) for a in mixes] + [full_spec(w) for w in w_outs]
                  + [row_spec(x), pl.BlockSpec((1, d), lambda i: (0, 0)), row_spec(p),
                     full_spec(w_gate), full_spec(w_proj)]),
        out_specs=row_spec(x),
        out_shape=jax.ShapeDtypeStruct((m, d), F32),
        compiler_params=pltpu.CompilerParams(dimension_semantics=("parallel",),
                                             vmem_limit_bytes=VMEM_LIMIT_BYTES),
        name="post",
    )(*mixes, *w_outs, x, post_g.reshape(1, d), p, w_gate, w_proj)


def _hgrn_kernel(pm_ref, lb_ref, nw_ref, sel_ref, out_ref, s_ref, kc_ref, *, first, n_heads, width):
    L = CHUNK

    @pl.when(pl.program_id(1) == 0)
    def _():
        s_ref[...] = jnp.zeros_like(s_ref)

    row = lax.broadcasted_iota(jnp.int32, (L, L), 0)
    col = lax.broadcasted_iota(jnp.int32, (L, L), 1)
    tril = _tril_bf16()
    diag_mask = jnp.logical_and(col // SUB == row // SUB, col <= row)
    nw = nw_ref[...]
    sel = sel_ref[...]

    def head_chain(sq, h):
        ls = slice(h * LANES, (h + 1) * LANES)
        qp = pm_ref[sq, :, h * LANES:(h + 1) * LANES]
        z = pm_ref[sq, :, width + h * LANES:width + (h + 1) * LANES]
        v = pm_ref[sq, :, 2 * width + h * LANES:2 * width + (h + 1) * LANES].astype(BF16)

        e = jnp.exp(-jnp.abs(z))
        inv = 1.0 / (1.0 + e)
        pos = z >= 0.0
        sig = jnp.where(pos, inv, e * inv)
        nsig = jnp.where(pos, e * inv, inv)
        if first:
            lf = jnp.minimum(z, 0.0) - jnp.log(1.0 + e)
            k = nsig
        else:
            lb = lb_ref[:, ls]
            lf = jnp.log(lb + (1.0 - lb) * sig)
            k = (1.0 - lb) * nsig
        q = _silu(qp)
        c = _cumsum_rows(tril, lf)
        yield
        c = c * LOG2E
        kc_ref[sq, h, 0] = k
        kc_ref[sq, h, 1] = c

        att_rows = [jnp.zeros((SUB, L), F32)]
        for i in range(1, L // SUB):
            ref_c = kc_ref[sq, h, 1, i * SUB:i * SUB + 1, :]
            kt = (k[0:i * SUB] * jnp.exp2(ref_c - c[0:i * SUB])).astype(BF16)
            kt = jnp.concatenate([kt, jnp.zeros((L - i * SUB, LANES), BF16)], axis=0)
            qt = (q[i * SUB:(i + 1) * SUB] * jnp.exp2(c[i * SUB:(i + 1) * SUB] - ref_c)).astype(BF16)
            att_rows.append(_dot_nt(qt, kt))

        p_rows = []
        for i in range(L // SUB):
            q_i = q[i * SUB:(i + 1) * SUB]
            c_i = c[i * SUB:(i + 1) * SUB]
            pieces = []
            for j in range(SUB):
                k_row = kc_ref[sq, h, 0, i * SUB + j:i * SUB + j + 1, :]
                c_row = kc_ref[sq, h, 1, i * SUB + j:i * SUB + j + 1, :]
                dec = jnp.exp2(jnp.minimum(c_i - c_row, 0.0))
                pieces.append((q_i * dec * k_row).astype(BF16))
            p_rows.append(jnp.concatenate(pieces, axis=1))
        att_diag = _dot(jnp.concatenate(p_rows, axis=0), sel)

        st = s_ref[sq, h]
        qs = _dot_nt((q * jnp.exp2(c)).astype(BF16), st.astype(BF16))
        c_last = kc_ref[sq, h, 1, L - 1:L, :]
        upd = _dot_tn(v, (k * jnp.exp2(c_last - c)).astype(BF16))
        yield
        att = jnp.concatenate(att_rows, axis=0) + jnp.where(diag_mask, att_diag[:, :L], 0.0)
        o = qs + _dot(att.astype(BF16), v)
        s_ref[sq, h] = jnp.exp2(c_last) * st + upd
        yield
        gb = pm_ref[sq, :, 3 * width + h * LANES:3 * width + (h + 1) * LANES]
        ms = jnp.mean(o * o, axis=-1, keepdims=True)
        out_ref[sq, :, ls] = (o * lax.rsqrt(ms + NORM_EPS) * nw * _silu(gb)).astype(BF16)

    _run_lockstep([functools.partial(head_chain, sq, h) for sq in range(SEQS_PER_STEP) for h in range(n_heads)],
                  HGRN_LOCKSTEP)


def _hgrn(proj_main, col_block, lb, norm_w, first, n_chunks, width):
    batch, seq, _ = proj_main.shape
    n_heads = width // LANES
    j = jnp.arange(SUB * LANES) // LANES
    s = jnp.arange(LANES)
    sel = ((s[None, :] % SUB == j[:, None]) & (s[None, :] < CHUNK)).astype(BF16)
    args = [proj_main, lb.reshape(1, width), norm_w.reshape(1, LANES), sel]
    return pl.pallas_call(
        functools.partial(_hgrn_kernel, first=first, n_heads=n_heads, width=width),
        grid=(batch // SEQS_PER_STEP, n_chunks),
        in_specs=[_seq_block(4 * width, n_chunks, col_block)] + [_const_block(a) for a in args[1:]],
        out_specs=_seq_block(width, n_chunks),
        out_shape=jax.ShapeDtypeStruct((batch, seq, width), BF16),
        scratch_shapes=[pltpu.VMEM((SEQS_PER_STEP, n_heads, LANES, LANES), F32),
                        pltpu.VMEM((SEQS_PER_STEP, n_heads, 2, CHUNK, LANES), F32)],
        compiler_params=pltpu.CompilerParams(dimension_semantics=("parallel", "arbitrary"),
                                             vmem_limit_bytes=VMEM_LIMIT_BYTES),
        name="hgrn2",
    )(*args)


def _rwkv_kernel(*refs, has_vres, n_pairs, width):
    (pm_ref, ps_ref, mum_ref, mus_ref, w0_ref, w2_ref, a0_ref, a2_ref, kk_ref, ka_ref, rk_ref,
     lnw_ref, lnb_ref) = refs[:13]
    if has_vres:
        vf_ref, v0_ref, v2_ref, out_ref, s_ref, cm_ref, cs_ref, fm_ref = refs[13:]
        vfo_ref = None
    else:
        out_ref, vfo_ref, s_ref, cm_ref, cs_ref, fm_ref = refs[13:]
    L = CHUNK

    @pl.when(pl.program_id(1) == 0)
    def _():
        s_ref[...] = jnp.zeros_like(s_ref)
        cm_ref[...] = jnp.zeros_like(cm_ref)
        cs_ref[...] = jnp.zeros_like(cs_ref)

    row1 = lax.broadcasted_iota(jnp.int32, (L, 1), 0)

    def token_shift(x, carry_row, mu):
        prev = jnp.where(row1 == 0, carry_row, pltpu.roll(x, 1, 0))
        return x + (prev - x) * mu

    tril = _tril_bf16()

    for sq in range(SEQS_PER_STEP):
        ps = ps_ref[sq]
        pss = token_shift(ps, cs_ref[sq, 0:1, :], mus_ref[...])
        cs_ref[sq, 0:1, :] = ps[L - 1:L, :]
        wd_t = jnp.tanh(pss[:, 0:LANES]).astype(BF16)
        ad = pss[:, LANES:2 * LANES].astype(BF16)
        lw_all = -math.exp(-0.5) * _sigmoid(w0_ref[...] + _dot(wd_t, w2_ref[...]))
        fm_ref[sq, 0] = lw_all * LOG2E
        fm_ref[sq, 1] = _cumsum_rows(tril, lw_all) * LOG2E
        fm_ref[sq, 2] = _sigmoid(a0_ref[...] + _dot(ad, a2_ref[...]))
        if has_vres:
            vd = pss[:, 2 * LANES:3 * LANES].astype(BF16)
            fm_ref[sq, 3] = _sigmoid(v0_ref[...] + _dot(vd, v2_ref[...]))

    _, m0, m1 = _lane_masks()
    strict2, incl2 = _pair_time_masks()
    rr = lax.broadcasted_iota(jnp.int32, (LANES, LANES), 0) // RWKV_HEAD_DIM
    cc = lax.broadcasted_iota(jnp.int32, (LANES, LANES), 1) // RWKV_HEAD_DIM
    bd_mask = (rr == cc).astype(F32)
    g128 = bd_mask.astype(BF16)
    inv_hd = 1.0 / RWKV_HEAD_DIM

    def stack_heads(x):
        return jnp.concatenate([x * m0, x * m1], axis=0)

    def pair_chain(sq, j):
        ls = slice(j * LANES, (j + 1) * LANES)

        def load(q):
            sl = slice(q * width + j * LANES, q * width + (j + 1) * LANES)
            x = pm_ref[sq, :, sl]
            y = token_shift(x, cm_ref[sq, 0:1, sl], mum_ref[:, sl])
            cm_ref[sq, 0:1, sl] = x[L - 1:L, :]
            return y

        r, k, v = load(0), load(1), load(2)
        lw, c, icl = fm_ref[sq, 0, :, ls], fm_ref[sq, 1, :, ls], fm_ref[sq, 2, :, ls]
        if has_vres:
            v = v + (vf_ref[sq, :, ls] - v) * fm_ref[sq, 3, :, ls]
        else:
            vfo_ref[sq, :, ls] = v
        kkr = k * kk_ref[:, ls]
        k2 = k * (1.0 + (icl - 1.0) * ka_ref[:, ls])
        sums = _dot(jnp.concatenate([kkr * kkr, r * k2 * rk_ref[:, ls]], axis=0).astype(BF16), g128)
        yield
        kkn = kkr * lax.rsqrt(sums[0:L] + 1e-12)
        bonus = sums[L:2 * L] * v
        b = kkn * icl
        c_last = c[L - 1:L, :]
        e_neg = jnp.exp2(-c)
        e_tail = jnp.exp2(c_last - c)
        a_t = -kkn * jnp.exp2(c - lw)
        r_t = r * jnp.exp2(c)
        lhs = jnp.concatenate([a_t, r_t], axis=0).astype(BF16)
        rhs_t = jnp.concatenate([stack_heads(b * e_neg), stack_heads(k2 * e_neg)], axis=0).astype(BF16)
        mm = _dot_nt(lhs, rhs_t)
        st = s_ref[sq, j]
        sr = _dot_nt(lhs, st.astype(BF16))
        yield
        m_ab = jnp.where(strict2, mm[0:L, 0:LANES], 0.0)
        m_ak = jnp.where(strict2, mm[0:L, LANES:2 * LANES], 0.0)
        p_rb = jnp.where(incl2, mm[L:2 * L, 0:LANES], 0.0)
        p_rk = jnp.where(incl2, mm[L:2 * L, LANES:2 * LANES], 0.0)
        v_bd = stack_heads(v).astype(BF16)
        rhs = sr[0:L] + _dot(m_ak.astype(BF16), v_bd)
        for step in _neumann_inverse_steps(stack_heads(m_ab)):
            if step is None:
                yield
            else:
                t_bd, = step
        t_cat = (t_bd[0:L] + t_bd[L:2 * L]).astype(BF16)
        u = _dot(t_cat, stack_heads(rhs).astype(BF16))
        yield
        u_bd = stack_heads(u).astype(BF16)
        o = sr[L:2 * L] + _dot(jnp.concatenate([p_rb, p_rk], axis=1).astype(BF16),
                               jnp.concatenate([u_bd, v_bd], axis=0))
        upd = _dot_tn(jnp.concatenate([u, v], axis=0).astype(BF16),
                      jnp.concatenate([b * e_tail, k2 * e_tail], axis=0).astype(BF16))
        yield
        s_ref[sq, j] = (jnp.exp2(c_last) * st + upd) * bd_mask
        mean = _dot(o.astype(BF16), g128) * inv_hd
        yield
        dlt = o - mean
        var = _dot((dlt * dlt).astype(BF16), g128) * inv_hd
        yield
        g = load(3)
        o_n = dlt * lax.rsqrt(var + RWKV_GN_EPS) * lnw_ref[:, ls] + lnb_ref[:, ls]
        out_ref[sq, :, ls] = ((o_n + bonus) * _silu(g)).astype(BF16)

    _run_lockstep([functools.partial(pair_chain, sq, j) for sq in range(SEQS_PER_STEP) for j in range(n_pairs)],
                  RWKV_LOCKSTEP)


def _rwkv(proj_main, proj_small, prm, v_first, n_chunks, width):
    batch, seq, _ = proj_main.shape
    n_pairs = width // LANES
    has_vres = v_first is not None
    ns = proj_small.shape[2]
    consts = [prm["mu_main"], prm["mu_small"], prm["w0"], prm["w2"], prm["a0"], prm["a2"],
              prm["k_k"], prm["k_a"], prm["r_k"], prm["lnx_w"], prm["lnx_b"]]
    args = [proj_main, proj_small] + consts
    in_specs = [_seq_block(4 * width, n_chunks), _seq_block(ns, n_chunks)] + [_const_block(a) for a in consts]
    if has_vres:
        args += [v_first, prm["v0"], prm["v2"]]
        in_specs += [_seq_block(width, n_chunks), _const_block(prm["v0"]), _const_block(prm["v2"])]
        out_specs = _seq_block(width, n_chunks)
        out_shape = jax.ShapeDtypeStruct((batch, seq, width), BF16)
    else:
        out_specs = [_seq_block(width, n_chunks), _seq_block(width, n_chunks)]
        out_shape = [jax.ShapeDtypeStruct((batch, seq, width), BF16), jax.ShapeDtypeStruct((batch, seq, width), F32)]
    return pl.pallas_call(
        functools.partial(_rwkv_kernel, has_vres=has_vres, n_pairs=n_pairs, width=width),
        grid=(batch // SEQS_PER_STEP, n_chunks),
        in_specs=in_specs,
        out_specs=out_specs,
        out_shape=out_shape,
        scratch_shapes=[pltpu.VMEM((SEQS_PER_STEP, n_pairs, LANES, LANES), F32),
                        pltpu.VMEM((SEQS_PER_STEP, CARRY_ROWS, 4 * width), F32),
                        pltpu.VMEM((SEQS_PER_STEP, CARRY_ROWS, ns), F32),
                        pltpu.VMEM((SEQS_PER_STEP, 4, CHUNK, width), F32)],
        compiler_params=pltpu.CompilerParams(dimension_semantics=("parallel", "arbitrary"),
                                             vmem_limit_bytes=VMEM_LIMIT_BYTES),
        name="rwkv7",
    )(*args)


def _gdn_kernel(qkv_ref, z_ref, sm_ref, cw_ref, al_ref, dtb_ref, nw_ref, out_ref,
                s_ref, carry_ref, act_ref, gate_ref, grow_ref, *, n_qk):
    L = CHUNK
    n_v = 2 * n_qk

    @pl.when(pl.program_id(1) == 0)
    def _():
        s_ref[...] = jnp.zeros_like(s_ref)
        carry_ref[...] = jnp.zeros_like(carry_ref)

    q_scale = GDN_HEAD_DIM ** -0.5
    tril = _tril_bf16()

    for sq in range(SEQS_PER_STEP):
        for gi in range(4 * n_qk):
            sl = slice(gi * LANES, (gi + 1) * LANES)
            x = qkv_ref[sq, :, sl]
            ext = jnp.concatenate([carry_ref[sq, :, sl], x], axis=0)
            acc = x * cw_ref[GDN_CONV - 1:GDN_CONV, sl]
            for d in range(1, GDN_CONV):
                acc = acc + pltpu.roll(ext, d, 0)[CARRY_ROWS:] * cw_ref[GDN_CONV - 1 - d:GDN_CONV - d, sl]
            carry_ref[sq, :, sl] = x[L - CARRY_ROWS:L, :]
            y = _silu(acc)
            if gi < 2 * n_qk:
                y = y * lax.rsqrt(jnp.sum(y * y, axis=-1, keepdims=True) + 1e-6)
                if gi < n_qk:
                    y = y * q_scale
            act_ref[sq, :, sl] = y

        sm = sm_ref[sq]
        beta = _sigmoid(sm)
        g_log = -jnp.exp(al_ref[...]) * _softplus(sm + dtb_ref[...])
        g2 = _cumsum_rows(tril, g_log) * LOG2E
        eg = jnp.exp2(g2)
        beg = beta * pltpu.roll(eg, LANES - n_v, 1)
        grow_ref[sq] = jnp.concatenate([g2, jnp.zeros((LANES - L, LANES), F32)], axis=0).T
        for h in range(n_v):
            gate_ref[sq, h, 0] = jnp.broadcast_to(g2[:, n_v + h:n_v + h + 1], (L, LANES))
            gate_ref[sq, h, 1] = jnp.broadcast_to(beta[:, h:h + 1], (L, LANES))
            gate_ref[sq, h, 2] = jnp.broadcast_to(eg[:, n_v + h:n_v + h + 1], (L, LANES))
            gate_ref[sq, h, 3] = jnp.broadcast_to(beg[:, h:h + 1], (L, LANES))

    lane, m0, m1 = _lane_masks()
    low = lane < LANES // 2
    strict2, incl2 = _pair_time_masks()
    nw = nw_ref[...]

    def pair_chain(sq, p):
        q_h = act_ref[sq, :, p * LANES:(p + 1) * LANES]
        k_h = act_ref[sq, :, (n_qk + p) * LANES:(n_qk + p + 1) * LANES]
        q_b, k_b = q_h.astype(BF16), k_h.astype(BF16)
        mm = _dot_nt(jnp.concatenate([q_b, k_b], axis=0), jnp.concatenate([k_b, k_b], axis=0))
        yield
        heads = (2 * p, 2 * p + 1)
        g_full = [gate_ref[sq, h, 0] for h in heads]
        b_full = [gate_ref[sq, h, 1] for h in heads]
        g_cat = jnp.where(low, g_full[0], g_full[1])
        b_cat = jnp.where(low, b_full[0], b_full[1])
        g_row = (grow_ref[sq, n_v + heads[0]:n_v + heads[0] + 1, :]
                 + pltpu.roll(grow_ref[sq, n_v + heads[1]:n_v + heads[1] + 1, :], LANES // 2, 1))
        dm = jnp.where(incl2, jnp.exp2(jnp.minimum(g_cat - g_row, 0.0)), 0.0)
        n_cat = jnp.where(strict2, -(b_cat * mm[L:2 * L] * dm), 0.0)
        qk_dm = mm[0:L] * dm
        for step in _neumann_inverse_steps(jnp.concatenate([n_cat * m0, n_cat * m1], axis=0)):
            if step is None:
                yield
            else:
                t_bd, = step

        rhs_rows = []
        for e, h in enumerate(heads):
            v_h = act_ref[sq, :, (2 * n_qk + h) * LANES:(2 * n_qk + h + 1) * LANES]
            rhs_rows.append(jnp.concatenate([b_full[e] * v_h, gate_ref[sq, h, 3] * k_h], axis=1))
        sol = _dot(t_bd.astype(BF16), jnp.concatenate(rhs_rows, axis=0).astype(BF16))
        yield
        states = [s_ref[sq, h] for h in heads]
        states_b = [s.astype(BF16) for s in states]
        sks = [_dot(sol[e * L:(e + 1) * L, LANES:2 * LANES].astype(BF16), states_b[e]) for e in range(2)]
        yield
        new_v = [sol[e * L:(e + 1) * L, 0:LANES] - sks[e] for e in range(2)]
        nv_stack = jnp.concatenate(new_v, axis=0).astype(BF16)
        outs, upds, g_lasts = [], [], []
        for e, h in enumerate(heads):
            mask = m0 if e == 0 else m1
            lhs = jnp.concatenate([q_h * gate_ref[sq, h, 2], qk_dm * mask], axis=1).astype(BF16)
            outs.append(_dot(lhs, jnp.concatenate([states_b[e], nv_stack], axis=0)))
            g_lasts.append(g_full[e][L - 1:L, :])
            k_tail = (k_h * jnp.exp2(g_lasts[e] - g_full[e])).astype(BF16)
            upds.append(_dot_tn(k_tail, nv_stack[e * L:(e + 1) * L]))
        yield
        for e, h in enumerate(heads):
            s_ref[sq, h] = jnp.exp2(g_lasts[e]) * states[e] + upds[e]
            o = outs[e]
            z = z_ref[sq, :, h * LANES:(h + 1) * LANES]
            ms = jnp.mean(o * o, axis=-1, keepdims=True)
            out_ref[sq, :, h * LANES:(h + 1) * LANES] = (o * lax.rsqrt(ms + NORM_EPS) * nw * _silu(z)).astype(BF16)

    _run_lockstep([functools.partial(pair_chain, sq, p) for sq in range(SEQS_PER_STEP) for p in range(n_qk)],
                  GDN_LOCKSTEP)


def _gdn(proj_main, proj_small, conv_w, a_log, dt_bias, norm_w, n_chunks, n_qk):
    batch, seq, _ = proj_main.shape
    n_v = 2 * n_qk
    qkv_w = 4 * n_qk * LANES
    z_w = n_v * LANES
    cw = jnp.concatenate([conv_w, jnp.zeros((CARRY_ROWS - GDN_CONV, qkv_w), F32)], axis=0)
    al = jnp.zeros((1, LANES), F32).at[0, n_v:2 * n_v].set(a_log)
    dtb = jnp.zeros((1, LANES), F32).at[0, n_v:2 * n_v].set(dt_bias)
    nw = norm_w.reshape(1, LANES)
    return pl.pallas_call(
        functools.partial(_gdn_kernel, n_qk=n_qk),
        grid=(batch // SEQS_PER_STEP, n_chunks),
        in_specs=[_seq_block(qkv_w, n_chunks), _seq_block(z_w, n_chunks, qkv_w // z_w), _seq_block(LANES, n_chunks),
                  _const_block(cw), _const_block(al), _const_block(dtb), _const_block(nw)],
        out_specs=_seq_block(z_w, n_chunks),
        out_shape=jax.ShapeDtypeStruct((batch, seq, z_w), BF16),
        scratch_shapes=[pltpu.VMEM((SEQS_PER_STEP, n_v, LANES, LANES), F32),
                        pltpu.VMEM((SEQS_PER_STEP, CARRY_ROWS, qkv_w), F32),
                        pltpu.VMEM((SEQS_PER_STEP, CHUNK, qkv_w), F32),
                        pltpu.VMEM((SEQS_PER_STEP, n_v, 4, CHUNK, LANES), F32),
                        pltpu.VMEM((SEQS_PER_STEP, LANES, LANES), F32)],
        compiler_params=pltpu.CompilerParams(dimension_semantics=("parallel", "arbitrary"),
                                             vmem_limit_bytes=VMEM_LIMIT_BYTES),
        name="gdn",
    )(proj_main, proj_main, proj_small, cw, al, dtb, nw)


def _pad_rows(w, rows):
    return jnp.concatenate([w, jnp.zeros((rows - w.shape[0], w.shape[1]), w.dtype)], axis=0)


def _even_params(j, width, ev_w_in, ev_mu, ev_w0, ev_w2, ev_a0, ev_a2, ev_k_k, ev_k_a, ev_r_k, ev_lnx_w, ev_lnx_b,
                 vr_w_down, vr_mu, vr_v0, vr_v2):
    d = ev_w_in.shape[1]
    w_in, mu = ev_w_in[j], ev_mu[j]
    rw = 4 * width
    lo_a, lo_b = rw + DECAY_RANK, rw + DECAY_RANK + ICL_RANK
    w_main = jnp.concatenate([w_in[:, :rw], w_in[:, lo_b:]], axis=1).astype(BF16)
    w_small = jnp.zeros((d, 3 * LANES), F32)
    w_small = w_small.at[:, 0:DECAY_RANK].set(w_in[:, rw:lo_a])
    w_small = w_small.at[:, LANES:LANES + ICL_RANK].set(w_in[:, lo_a:lo_b])
    mu_small = jnp.zeros((1, 3 * LANES), F32)
    mu_small = mu_small.at[0, 0:DECAY_RANK].set(mu[rw:lo_a])
    mu_small = mu_small.at[0, LANES:LANES + ICL_RANK].set(mu[lo_a:lo_b])
    row = lambda a: a.reshape(1, width)
    prm = dict(mu_main=mu[:rw].reshape(1, rw), w0=row(ev_w0[j]), w2=_pad_rows(ev_w2[j], LANES).astype(BF16),
               a0=row(ev_a0[j]), a2=_pad_rows(ev_a2[j], LANES).astype(BF16), k_k=row(ev_k_k[j]), k_a=row(ev_k_a[j]),
               r_k=row(ev_r_k[j]), lnx_w=row(ev_lnx_w[j]), lnx_b=row(ev_lnx_b[j]))
    if j > 0:
        w_small = w_small.at[:, 2 * LANES:2 * LANES + VRES_RANK].set(vr_w_down[j - 1])
        mu_small = mu_small.at[0, 2 * LANES:2 * LANES + VRES_RANK].set(vr_mu[j - 1])
        prm["v0"] = row(vr_v0[j - 1])
        prm["v2"] = _pad_rows(vr_v2[j - 1], LANES).astype(BF16)
    prm["mu_small"] = mu_small
    return w_main, w_small.astype(BF16), prm


def kernel(x, p, pre_g, post_g, ple_w_proj, ple_w_gate, ev_w_in, ev_mu, ev_w0, ev_w2, ev_a0, ev_a2, ev_k_k, ev_k_a, ev_r_k, ev_lnx_w, ev_lnx_b, vr_w_down, vr_mu, vr_v0, vr_v2, hg_lb, hg_norm_w, ev_w_out, od_w_in, od_conv_w, od_a_log, od_dt_bias, od_norm_w, od_w_out):
    batch, seq, d = x.shape
    assert batch % SEQS_PER_STEP == 0 and seq % CHUNK == 0
    depth = pre_g.shape[0]
    m = batch * seq
    n_chunks = seq // CHUNK
    width = ev_w0.shape[1]
    n_v = od_a_log.shape[1]
    n_qk = n_v // 2
    per_seq = lambda a: a.reshape(batch, seq, a.shape[-1])
    flat = lambda a: a.reshape(m, a.shape[-1])

    lb_all = jnp.cumsum(jax.nn.softmax(hg_lb.astype(F32), axis=0), axis=0)
    lb_all = lb_all - lb_all[0]

    xf = x.reshape(m, d)
    v_first = None
    for i in range(depth):
        j = i // 2
        if i % 2 == 0:
            w_main, w_small, prm = _even_params(j, width, ev_w_in, ev_mu, ev_w0, ev_w2, ev_a0, ev_a2, ev_k_k, ev_k_a,
                                                ev_r_k, ev_lnx_w, ev_lnx_b, vr_w_down, vr_mu, vr_v0, vr_v2)
            proj_main, proj_small = _norm_proj(xf, pre_g[i], w_main, w_small)
            proj_main, proj_small = per_seq(proj_main), per_seq(proj_small)
            if j == 0:
                mix_a, v_first = _rwkv(proj_main, proj_small, prm, None, n_chunks, width)
            else:
                mix_a = _rwkv(proj_main, proj_small, prm, v_first, n_chunks, width)
            mix_b = _hgrn(proj_main, 1, lb_all[j], hg_norm_w[j], j == 0, n_chunks, width)
            w_out = ev_w_out[j].astype(BF16)
            mixes, w_outs = [flat(mix_a), flat(mix_b)], [w_out[:width], w_out[width:]]
        else:
            w_in = od_w_in[j]
            n_main = (4 * n_qk + n_v) * LANES
            w_small = jnp.zeros((d, LANES), F32).at[:, 0:2 * n_v].set(w_in[:, n_main:])
            proj_main, proj_small = _norm_proj(xf, pre_g[i], w_in[:, :n_main].astype(BF16), w_small.astype(BF16))
            mix = _gdn(per_seq(proj_main), per_seq(proj_small), od_conv_w[j], od_a_log[j], od_dt_bias[j],
                       od_norm_w[j], n_chunks, n_qk)
            mixes, w_outs = [flat(mix)], [od_w_out[j].astype(BF16)]
        xf = _post(mixes, w_outs, xf, post_g[i], p[i].reshape(m, -1), ple_w_gate[i].astype(BF16),
                   ple_w_proj[i].astype(BF16))
    return xf.reshape(batch, seq, d)
```

```python
import functools
import math

import jax
import jax.numpy as jnp
from jax import lax
from jax.experimental import pallas as pl
from jax.experimental.pallas import tpu as pltpu

F32 = jnp.float32
BF16 = jnp.bfloat16

LANES = 128
CARRY_ROWS = 8
VMEM_LIMIT_BYTES = 56 * 1024 * 1024

CHUNK = 64
SUB = 16
SEQS_PER_STEP = 2
HGRN_LOCKSTEP = 16
RWKV_LOCKSTEP = 16
GDN_LOCKSTEP = 8
NORM_EPS = 1e-6
RWKV_HEAD_DIM = 64
RWKV_GN_EPS = 64e-5
DECAY_RANK = 64
ICL_RANK = 64
VRES_RANK = 32
GDN_HEAD_DIM = 128
GDN_CONV = 4
LOG2E = math.log2(math.e)


def _dot(a, b):
    return jnp.dot(a, b, preferred_element_type=F32)


def _dot_nt(a, b):
    return lax.dot_general(a, b, (((1,), (1,)), ((), ())), preferred_element_type=F32)


def _dot_tn(a, b):
    return lax.dot_general(a, b, (((0,), (0,)), ((), ())), preferred_element_type=F32)


def _split3(x):
    hi = x.astype(BF16)
    r1 = x - hi.astype(F32)
    mid = r1.astype(BF16)
    lo = (r1 - mid.astype(F32)).astype(BF16)
    return hi, mid, lo


def _cumsum_rows(tril, x):
    n = x.shape[1]
    cs = _dot(tril, jnp.concatenate(_split3(x), axis=1))
    return (cs[:, 0:n] + cs[:, n:2 * n]) + cs[:, 2 * n:3 * n]


def _sigmoid(x):
    return 1.0 / (1.0 + jnp.exp(-x))


def _silu(x):
    return x * _sigmoid(x)


def _softplus(x):
    return jnp.maximum(x, 0.0) + jnp.log(1.0 + jnp.exp(-jnp.abs(x)))


def _eye(size):
    return (lax.broadcasted_iota(jnp.int32, (size, size), 0)
            == lax.broadcasted_iota(jnp.int32, (size, size), 1)).astype(F32)


def _neumann_inverse_steps(n):
    nb = n.astype(BF16)
    p = _dot(nb, nb)
    yield None
    x = _eye(n.shape[0]) + n
    for _ in range(int(math.log2(CHUNK)) - 2):
        pb = p.astype(BF16)
        xp = _dot(x.astype(BF16), pb)
        p2 = _dot(pb, pb)
        yield None
        x = x + xp
        p = p2
    xp = _dot(x.astype(BF16), p.astype(BF16))
    yield None
    yield (x + xp,)


def _run_lockstep(chain_makers, group):
    chain_makers = list(chain_makers)
    for start in range(0, len(chain_makers), group):
        chains = [make() for make in chain_makers[start:start + group]]
        while chains:
            alive = []
            for ch in chains:
                try:
                    next(ch)
                    alive.append(ch)
                except StopIteration:
                    pass
            chains = alive


def _lane_masks():
    lane = lax.broadcasted_iota(jnp.int32, (1, LANES), 1)
    m0 = (lane < LANES // 2).astype(F32)
    return lane, m0, 1.0 - m0


def _pair_time_masks():
    row = lax.broadcasted_iota(jnp.int32, (CHUNK, LANES), 0)
    src = lax.broadcasted_iota(jnp.int32, (CHUNK, LANES), 1) & (CHUNK - 1)
    return src < row, src <= row


def _tril_bf16():
    row = lax.broadcasted_iota(jnp.int32, (CHUNK, CHUNK), 0)
    col = lax.broadcasted_iota(jnp.int32, (CHUNK, CHUNK), 1)
    return (row >= col).astype(BF16)


def _seq_block(width, n_chunks, col_block=0):
    return pl.BlockSpec((SEQS_PER_STEP, CHUNK, width), lambda b, c: (b, c, col_block))


def _const_block(a):
    return pl.BlockSpec(a.shape, lambda b, c: (0,) * a.ndim)


def _norm_proj_kernel(x_ref, g_ref, wm_ref, ws_ref, om_ref, os_ref, h_ref):
    @pl.when(pl.program_id(1) == 0)
    def _():
        x = x_ref[...]
        ms = jnp.mean(x * x, axis=-1, keepdims=True)
        h = (x * lax.rsqrt(ms + NORM_EPS) * g_ref[...]).astype(BF16)
        h_ref[...] = h
        os_ref[...] = _dot(h, ws_ref[...])

    om_ref[...] = _dot(h_ref[...], wm_ref[...])


def _norm_proj(x, g, wm, ws):
    m, d = x.shape
    nm, ns = wm.shape[1], ws.shape[1]
    tm = min(1024, m)
    tn = min(1024, nm)
    return pl.pallas_call(
        _norm_proj_kernel,
        grid=(m // tm, nm // tn),
        in_specs=[pl.BlockSpec((tm, d), lambda i, j: (i, 0)),
                  pl.BlockSpec((1, d), lambda i, j: (0, 0)),
                  pl.BlockSpec((d, tn), lambda i, j: (0, j)),
                  pl.BlockSpec((d, ns), lambda i, j: (0, 0))],
        out_specs=[pl.BlockSpec((tm, tn), lambda i, j: (i, j)),
                   pl.BlockSpec((tm, ns), lambda i, j: (i, 0))],
        out_shape=[jax.ShapeDtypeStruct((m, nm), F32), jax.ShapeDtypeStruct((m, ns), F32)],
        scratch_shapes=[pltpu.VMEM((tm, d), BF16)],
        compiler_params=pltpu.CompilerParams(dimension_semantics=("parallel", "arbitrary"),
                                             vmem_limit_bytes=VMEM_LIMIT_BYTES),
        name="norm_proj",
    )(x, g.reshape(1, d), wm, ws)


def _post_kernel(*refs, n_mix):
    mix_refs, w_refs = refs[:n_mix], refs[n_mix:2 * n_mix]
    x_ref, pg_ref, p_ref, wg_ref, wp_ref, o_ref = refs[2 * n_mix:]
    y = _dot(mix_refs[0][...], w_refs[0][...])
    for mr, wr in zip(mix_refs[1:], w_refs[1:]):
        y = y + _dot(mr[...], wr[...])
    ms = jnp.mean(y * y, axis=-1, keepdims=True)
    x1 = x_ref[...] + y * lax.rsqrt(ms + NORM_EPS) * pg_ref[...]
    gate = _sigmoid(_dot(x1.astype(BF16), wg_ref[...]))
    o_ref[...] = x1 + gate * _dot(p_ref[...].astype(BF16), wp_ref[...])


def _post(mixes, w_out, x, post_g, p, w_gate, w_proj):
    m, d = x.shape
    tm = min(256, m)
    n_mix = len(mixes)
    k_mix = w_out.shape[0] // n_mix
    assert all(a.shape[1] == k_mix for a in mixes)
    row_spec = lambda a: pl.BlockSpec((tm, a.shape[1]), lambda i: (i, 0))
    full_spec = lambda a: pl.BlockSpec(a.shape, lambda i: (0, 0))
    w_specs = [pl.BlockSpec((k_mix, d), lambda i, blk=blk: (blk, 0)) for blk in range(n_mix)]
    w_outs = [w_out] * n_mix
    return pl.pallas_call(
        functools.partial(_post_kernel, n_mix=n_mix),
        grid=(m // tm,),
        in_specs=([row_spec(a) for a in mixes] + w_specs
                  + [row_spec(x), pl.BlockSpec((1, d), lambda i: (0, 0)), row_spec(p),
                     full_spec(w_gate), full_spec(w_proj)]),
        out_specs=row_spec(x),
        out_shape=jax.ShapeDtypeStruct((m, d), F32),
        compiler_params=pltpu.CompilerParams(dimension_semantics=("parallel",),
                                             vmem_limit_bytes=VMEM_LIMIT_BYTES),
        name="post",
    )(*mixes, *w_outs, x, post_g.reshape(1, d), p, w_gate, w_proj)


def _hgrn_kernel(pm_ref, lb_ref, nw_ref, sel_ref, out_ref, s_ref, kc_ref, *, first, n_heads, width):
    L = CHUNK

    @pl.when(pl.program_id(1) == 0)
    def _():
        s_ref[...] = jnp.zeros_like(s_ref)

    row = lax.broadcasted_iota(jnp.int32, (L, L), 0)
    col = lax.broadcasted_iota(jnp.int32, (L, L), 1)
    tril = _tril_bf16()
    diag_mask = jnp.logical_and(col // SUB == row // SUB, col <= row)
    nw = nw_ref[...]
    sel = sel_ref[...]

    def head_chain(sq, h):
        ls = slice(h * LANES, (h + 1) * LANES)
        qp = pm_ref[sq, :, h * LANES:(h + 1) * LANES]
        z = pm_ref[sq, :, width + h * LANES:width + (h + 1) * LANES]
        v = pm_ref[sq, :, 2 * width + h * LANES:2 * width + (h + 1) * LANES].astype(BF16)

        e = jnp.exp(-jnp.abs(z))
        inv = 1.0 / (1.0 + e)
        pos = z >= 0.0
        sig = jnp.where(pos, inv, e * inv)
        nsig = jnp.where(pos, e * inv, inv)
        if first:
            lf = jnp.minimum(z, 0.0) - jnp.log(1.0 + e)
            k = nsig
        else:
            lb = lb_ref[:, ls]
            lf = jnp.log(lb + (1.0 - lb) * sig)
            k = (1.0 - lb) * nsig
        q = _silu(qp)
        c = _cumsum_rows(tril, lf)
        yield
        c = c * LOG2E
        kc_ref[sq, h, 0] = k
        kc_ref[sq, h, 1] = c

        att_rows = [jnp.zeros((SUB, L), F32)]
        for i in range(1, L // SUB):
            ref_c = kc_ref[sq, h, 1, i * SUB:i * SUB + 1, :]
            kt = (k[0:i * SUB] * jnp.exp2(ref_c - c[0:i * SUB])).astype(BF16)
            kt = jnp.concatenate([kt, jnp.zeros((L - i * SUB, LANES), BF16)], axis=0)
            qt = (q[i * SUB:(i + 1) * SUB] * jnp.exp2(c[i * SUB:(i + 1) * SUB] - ref_c)).astype(BF16)
            att_rows.append(_dot_nt(qt, kt))

        p_rows = []
        for i in range(L // SUB):
            q_i = q[i * SUB:(i + 1) * SUB]
            c_i = c[i * SUB:(i + 1) * SUB]
            pieces = []
            for j in range(SUB):
                k_row = kc_ref[sq, h, 0, i * SUB + j:i * SUB + j + 1, :]
                c_row = kc_ref[sq, h, 1, i * SUB + j:i * SUB + j + 1, :]
                dec = jnp.exp2(jnp.minimum(c_i - c_row, 0.0))
                pieces.append((q_i * dec * k_row).astype(BF16))
            p_rows.append(jnp.concatenate(pieces, axis=1))
        att_diag = _dot(jnp.concatenate(p_rows, axis=0), sel)

        st = s_ref[sq, h]
        qs = _dot_nt((q * jnp.exp2(c)).astype(BF16), st.astype(BF16))
        c_last = kc_ref[sq, h, 1, L - 1:L, :]
        upd = _dot_tn(v, (k * jnp.exp2(c_last - c)).astype(BF16))
        yield
        att = jnp.concatenate(att_rows, axis=0) + jnp.where(diag_mask, att_diag[:, :L], 0.0)
        o = qs + _dot(att.astype(BF16), v)
        s_ref[sq, h] = jnp.exp2(c_last) * st + upd
        yield
        gb = pm_ref[sq, :, 3 * width + h * LANES:3 * width + (h + 1) * LANES]
        ms = jnp.mean(o * o, axis=-1, keepdims=True)
        out_ref[sq, :, ls] = (o * lax.rsqrt(ms + NORM_EPS) * nw * _silu(gb)).astype(BF16)

    _run_lockstep([functools.partial(head_chain, sq, h) for sq in range(SEQS_PER_STEP) for h in range(n_heads)],
                  HGRN_LOCKSTEP)


def _hgrn(proj_main, col_block, lb, norm_w, first, n_chunks, width):
    batch, seq, _ = proj_main.shape
    n_heads = width // LANES
    j = jnp.arange(SUB * LANES) // LANES
    s = jnp.arange(LANES)
    sel = ((s[None, :] % SUB == j[:, None]) & (s[None, :] < CHUNK)).astype(BF16)
    args = [proj_main, lb.reshape(1, width), norm_w.reshape(1, LANES), sel]
    return pl.pallas_call(
        functools.partial(_hgrn_kernel, first=first, n_heads=n_heads, width=width),
        grid=(batch // SEQS_PER_STEP, n_chunks),
        in_specs=[_seq_block(4 * width, n_chunks, col_block)] + [_const_block(a) for a in args[1:]],
        out_specs=_seq_block(width, n_chunks),
        out_shape=jax.ShapeDtypeStruct((batch, seq, width), BF16),
        scratch_shapes=[pltpu.VMEM((SEQS_PER_STEP, n_heads, LANES, LANES), F32),
                        pltpu.VMEM((SEQS_PER_STEP, n_heads, 2, CHUNK, LANES), F32)],
        compiler_params=pltpu.CompilerParams(dimension_semantics=("parallel", "arbitrary"),
                                             vmem_limit_bytes=VMEM_LIMIT_BYTES),
        name="hgrn2",
    )(*args)


def _rwkv_kernel(*refs, has_vres, n_pairs, width):
    (pm_ref, ps_ref, mum_ref, mus_ref, w0_ref, w2_ref, a0_ref, a2_ref, kk_ref, ka_ref, rk_ref,
     lnw_ref, lnb_ref) = refs[:13]
    if has_vres:
        vf_ref, v0_ref, v2_ref, out_ref, s_ref, cm_ref, cs_ref, fm_ref = refs[13:]
        vfo_ref = None
    else:
        out_ref, vfo_ref, s_ref, cm_ref, cs_ref, fm_ref = refs[13:]
    L = CHUNK

    @pl.when(pl.program_id(1) == 0)
    def _():
        s_ref[...] = jnp.zeros_like(s_ref)
        cm_ref[...] = jnp.zeros_like(cm_ref)
        cs_ref[...] = jnp.zeros_like(cs_ref)

    row1 = lax.broadcasted_iota(jnp.int32, (L, 1), 0)

    def token_shift(x, carry_row, mu):
        prev = jnp.where(row1 == 0, carry_row, pltpu.roll(x, 1, 0))
        return x + (prev - x) * mu

    tril = _tril_bf16()

    for sq in range(SEQS_PER_STEP):
        ps = ps_ref[sq]
        pss = token_shift(ps, cs_ref[sq, 0:1, :], mus_ref[...])
        cs_ref[sq, 0:1, :] = ps[L - 1:L, :]
        wd_t = jnp.tanh(pss[:, 0:LANES]).astype(BF16)
        ad = pss[:, LANES:2 * LANES].astype(BF16)
        lw_all = -math.exp(-0.5) * _sigmoid(w0_ref[...] + _dot(wd_t, w2_ref[...]))
        fm_ref[sq, 0] = lw_all * LOG2E
        fm_ref[sq, 1] = _cumsum_rows(tril, lw_all) * LOG2E
        fm_ref[sq, 2] = _sigmoid(a0_ref[...] + _dot(ad, a2_ref[...]))
        if has_vres:
            vd = pss[:, 2 * LANES:3 * LANES].astype(BF16)
            fm_ref[sq, 3] = _sigmoid(v0_ref[...] + _dot(vd, v2_ref[...]))

    _, m0, m1 = _lane_masks()
    strict2, incl2 = _pair_time_masks()
    rr = lax.broadcasted_iota(jnp.int32, (LANES, LANES), 0) // RWKV_HEAD_DIM
    cc = lax.broadcasted_iota(jnp.int32, (LANES, LANES), 1) // RWKV_HEAD_DIM
    bd_mask = (rr == cc).astype(F32)
    g128 = bd_mask.astype(BF16)
    inv_hd = 1.0 / RWKV_HEAD_DIM

    def stack_heads(x):
        return jnp.concatenate([x * m0, x * m1], axis=0)

    def pair_chain(sq, j):
        ls = slice(j * LANES, (j + 1) * LANES)

        def load(q):
            sl = slice(q * width + j * LANES, q * width + (j + 1) * LANES)
            x = pm_ref[sq, :, sl]
            y = token_shift(x, cm_ref[sq, 0:1, sl], mum_ref[:, sl])
            cm_ref[sq, 0:1, sl] = x[L - 1:L, :]
            return y

        r, k, v = load(0), load(1), load(2)
        lw, c, icl = fm_ref[sq, 0, :, ls], fm_ref[sq, 1, :, ls], fm_ref[sq, 2, :, ls]
        if has_vres:
            v = v + (vf_ref[sq, :, ls] - v) * fm_ref[sq, 3, :, ls]
        else:
            vfo_ref[sq, :, ls] = v
        kkr = k * kk_ref[:, ls]
        k2 = k * (1.0 + (icl - 1.0) * ka_ref[:, ls])
        sums = _dot(jnp.concatenate([kkr * kkr, r * k2 * rk_ref[:, ls]], axis=0).astype(BF16), g128)
        yield
        kkn = kkr * lax.rsqrt(sums[0:L] + 1e-12)
        bonus = sums[L:2 * L] * v
        b = kkn * icl
        c_last = c[L - 1:L, :]
        e_neg = jnp.exp2(-c)
        e_tail = jnp.exp2(c_last - c)
        a_t = -kkn * jnp.exp2(c - lw)
        r_t = r * jnp.exp2(c)
        lhs = jnp.concatenate([a_t, r_t], axis=0).astype(BF16)
        rhs_t = jnp.concatenate([stack_heads(b * e_neg), stack_heads(k2 * e_neg)], axis=0).astype(BF16)
        mm = _dot_nt(lhs, rhs_t)
        st = s_ref[sq, j]
        sr = _dot_nt(lhs, st.astype(BF16))
        yield
        m_ab = jnp.where(strict2, mm[0:L, 0:LANES], 0.0)
        m_ak = jnp.where(strict2, mm[0:L, LANES:2 * LANES], 0.0)
        p_rb = jnp.where(incl2, mm[L:2 * L, 0:LANES], 0.0)
        p_rk = jnp.where(incl2, mm[L:2 * L, LANES:2 * LANES], 0.0)
        v_bd = stack_heads(v).astype(BF16)
        rhs = sr[0:L] + _dot(m_ak.astype(BF16), v_bd)
        for step in _neumann_inverse_steps(stack_heads(m_ab)):
            if step is None:
                yield
            else:
                t_bd, = step
        t_cat = (t_bd[0:L] + t_bd[L:2 * L]).astype(BF16)
        u = _dot(t_cat, stack_heads(rhs).astype(BF16))
        yield
        u_bd = stack_heads(u).astype(BF16)
        o = sr[L:2 * L] + _dot(jnp.concatenate([p_rb, p_rk], axis=1).astype(BF16),
                               jnp.concatenate([u_bd, v_bd], axis=0))
        upd = _dot_tn(jnp.concatenate([u, v], axis=0).astype(BF16),
                      jnp.concatenate([b * e_tail, k2 * e_tail], axis=0).astype(BF16))
        yield
        s_ref[sq, j] = (jnp.exp2(c_last) * st + upd) * bd_mask
        mean = _dot(o.astype(BF16), g128) * inv_hd
        yield
        dlt = o - mean
        var = _dot((dlt * dlt).astype(BF16), g128) * inv_hd
        yield
        g = load(3)
        o_n = dlt * lax.rsqrt(var + RWKV_GN_EPS) * lnw_ref[:, ls] + lnb_ref[:, ls]
        out_ref[sq, :, ls] = ((o_n + bonus) * _silu(g)).astype(BF16)

    _run_lockstep([functools.partial(pair_chain, sq, j) for sq in range(SEQS_PER_STEP) for j in range(n_pairs)],
                  RWKV_LOCKSTEP)


def _rwkv(proj_main, proj_small, prm, v_first, n_chunks, width):
    batch, seq, _ = proj_main.shape
    n_pairs = width // LANES
    has_vres = v_first is not None
    ns = proj_small.shape[2]
    consts = [prm["mu_main"], prm["mu_small"], prm["w0"], prm["w2"], prm["a0"], prm["a2"],
              prm["k_k"], prm["k_a"], prm["r_k"], prm["lnx_w"], prm["lnx_b"]]
    args = [proj_main, proj_small] + consts
    in_specs = [_seq_block(4 * width, n_chunks), _seq_block(ns, n_chunks)] + [_const_block(a) for a in consts]
    if has_vres:
        args += [v_first, prm["v0"], prm["v2"]]
        in_specs += [_seq_block(width, n_chunks), _const_block(prm["v0"]), _const_block(prm["v2"])]
        out_specs = _seq_block(width, n_chunks)
        out_shape = jax.ShapeDtypeStruct((batch, seq, width), BF16)
    else:
        out_specs = [_seq_block(width, n_chunks), _seq_block(width, n_chunks)]
        out_shape = [jax.ShapeDtypeStruct((batch, seq, width), BF16), jax.ShapeDtypeStruct((batch, seq, width), F32)]
    return pl.pallas_call(
        functools.partial(_rwkv_kernel, has_vres=has_vres, n_pairs=n_pairs, width=width),
        grid=(batch // SEQS_PER_STEP, n_chunks),
        in_specs=in_specs,
        out_specs=out_specs,
        out_shape=out_shape,
        scratch_shapes=[pltpu.VMEM((SEQS_PER_STEP, n_pairs, LANES, LANES), F32),
                        pltpu.VMEM((SEQS_PER_STEP, CARRY_ROWS, 4 * width), F32),
                        pltpu.VMEM((SEQS_PER_STEP, CARRY_ROWS, ns), F32),
                        pltpu.VMEM((SEQS_PER_STEP, 4, CHUNK, width), F32)],
        compiler_params=pltpu.CompilerParams(dimension_semantics=("parallel", "arbitrary"),
                                             vmem_limit_bytes=VMEM_LIMIT_BYTES),
        name="rwkv7",
    )(*args)


def _gdn_kernel(qkv_ref, z_ref, sm_ref, cw_ref, al_ref, dtb_ref, nw_ref, out_ref,
                s_ref, carry_ref, act_ref, gate_ref, grow_ref, *, n_qk):
    L = CHUNK
    n_v = 2 * n_qk

    @pl.when(pl.program_id(1) == 0)
    def _():
        s_ref[...] = jnp.zeros_like(s_ref)
        carry_ref[...] = jnp.zeros_like(carry_ref)

    q_scale = GDN_HEAD_DIM ** -0.5
    tril = _tril_bf16()

    for sq in range(SEQS_PER_STEP):
        for gi in range(4 * n_qk):
            sl = slice(gi * LANES, (gi + 1) * LANES)
            x = qkv_ref[sq, :, sl]
            ext = jnp.concatenate([carry_ref[sq, :, sl], x], axis=0)
            acc = x * cw_ref[GDN_CONV - 1:GDN_CONV, sl]
            for d in range(1, GDN_CONV):
                acc = acc + pltpu.roll(ext, d, 0)[CARRY_ROWS:] * cw_ref[GDN_CONV - 1 - d:GDN_CONV - d, sl]
            carry_ref[sq, :, sl] = x[L - CARRY_ROWS:L, :]
            y = _silu(acc)
            if gi < 2 * n_qk:
                y = y * lax.rsqrt(jnp.sum(y * y, axis=-1, keepdims=True) + 1e-6)
                if gi < n_qk:
                    y = y * q_scale
            act_ref[sq, :, sl] = y

        sm = sm_ref[sq]
        beta = _sigmoid(sm)
        g_log = -jnp.exp(al_ref[...]) * _softplus(sm + dtb_ref[...])
        g2 = _cumsum_rows(tril, g_log) * LOG2E
        eg = jnp.exp2(g2)
        beg = beta * pltpu.roll(eg, LANES - n_v, 1)
        grow_ref[sq] = jnp.concatenate([g2, jnp.zeros((LANES - L, LANES), F32)], axis=0).T
        for h in range(n_v):
            gate_ref[sq, h, 0] = jnp.broadcast_to(g2[:, n_v + h:n_v + h + 1], (L, LANES))
            gate_ref[sq, h, 1] = jnp.broadcast_to(beta[:, h:h + 1], (L, LANES))
            gate_ref[sq, h, 2] = jnp.broadcast_to(eg[:, n_v + h:n_v + h + 1], (L, LANES))
            gate_ref[sq, h, 3] = jnp.broadcast_to(beg[:, h:h + 1], (L, LANES))

    lane, m0, m1 = _lane_masks()
    low = lane < LANES // 2
    strict2, incl2 = _pair_time_masks()
    nw = nw_ref[...]

    def pair_chain(sq, p):
        q_h = act_ref[sq, :, p * LANES:(p + 1) * LANES]
        k_h = act_ref[sq, :, (n_qk + p) * LANES:(n_qk + p + 1) * LANES]
        q_b, k_b = q_h.astype(BF16), k_h.astype(BF16)
        heads = (2 * p, 2 * p + 1)
        mm = _dot_nt(jnp.concatenate([q_b, k_b], axis=0), jnp.concatenate([k_b, k_b], axis=0))
        states = [s_ref[sq, h] for h in heads]
        states_b = [s.astype(BF16) for s in states]
        ks = [_dot((gate_ref[sq, h, 3] * k_h).astype(BF16), states_b[e]) for e, h in enumerate(heads)]
        yield
        g_full = [gate_ref[sq, h, 0] for h in heads]
        b_full = [gate_ref[sq, h, 1] for h in heads]
        g_cat = jnp.where(low, g_full[0], g_full[1])
        b_cat = jnp.where(low, b_full[0], b_full[1])
        g_row = (grow_ref[sq, n_v + heads[0]:n_v + heads[0] + 1, :]
                 + pltpu.roll(grow_ref[sq, n_v + heads[1]:n_v + heads[1] + 1, :], LANES // 2, 1))
        dm = jnp.where(incl2, jnp.exp2(jnp.minimum(g_cat - g_row, 0.0)), 0.0)
        n_cat = jnp.where(strict2, -(b_cat * mm[L:2 * L] * dm), 0.0)
        qk_dm = mm[0:L] * dm
        rhs = []
        for e, h in enumerate(heads):
            v_h = act_ref[sq, :, (2 * n_qk + h) * LANES:(2 * n_qk + h + 1) * LANES]
            rhs.append(b_full[e] * v_h - ks[e])
        rhs = jnp.concatenate(rhs, axis=0).astype(BF16)
        for step in _neumann_inverse_steps(jnp.concatenate([n_cat * m0, n_cat * m1], axis=0)):
            if step is None:
                yield
            else:
                t_bd, = step
        new_v = _dot(t_bd.astype(BF16), rhs)
        yield
        nv_stack = new_v.astype(BF16)
        outs, upds, g_lasts = [], [], []
        for e, h in enumerate(heads):
            mask = m0 if e == 0 else m1
            lhs = jnp.concatenate([q_h * gate_ref[sq, h, 2], qk_dm * mask], axis=1).astype(BF16)
            outs.append(_dot(lhs, jnp.concatenate([states_b[e], nv_stack], axis=0)))
            g_lasts.append(g_full[e][L - 1:L, :])
            k_tail = (k_h * jnp.exp2(g_lasts[e] - g_full[e])).astype(BF16)
            upds.append(_dot_tn(k_tail, nv_stack[e * L:(e + 1) * L]))
        yield
        for e, h in enumerate(heads):
            s_ref[sq, h] = jnp.exp2(g_lasts[e]) * states[e] + upds[e]
            o = outs[e]
            z = z_ref[sq, :, h * LANES:(h + 1) * LANES]
            ms = jnp.mean(o * o, axis=-1, keepdims=True)
            out_ref[sq, :, h * LANES:(h + 1) * LANES] = (o * lax.rsqrt(ms + NORM_EPS) * nw * _silu(z)).astype(BF16)

    _run_lockstep([functools.partial(pair_chain, sq, p) for sq in range(SEQS_PER_STEP) for p in range(n_qk)],
                  GDN_LOCKSTEP)


def _gdn(proj_main, proj_small, conv_w, a_log, dt_bias, norm_w, n_chunks, n_qk):
    batch, seq, _ = proj_main.shape
    n_v = 2 * n_qk
    qkv_w = 4 * n_qk * LANES
    z_w = n_v * LANES
    cw = jnp.concatenate([conv_w, jnp.zeros((CARRY_ROWS - GDN_CONV, qkv_w), F32)], axis=0)
    al = jnp.zeros((1, LANES), F32).at[0, n_v:2 * n_v].set(a_log)
    dtb = jnp.zeros((1, LANES), F32).at[0, n_v:2 * n_v].set(dt_bias)
    nw = norm_w.reshape(1, LANES)
    return pl.pallas_call(
        functools.partial(_gdn_kernel, n_qk=n_qk),
        grid=(batch // SEQS_PER_STEP, n_chunks),
        in_specs=[_seq_block(qkv_w, n_chunks), _seq_block(z_w, n_chunks, qkv_w // z_w), _seq_block(LANES, n_chunks),
                  _const_block(cw), _const_block(al), _const_block(dtb), _const_block(nw)],
        out_specs=_seq_block(z_w, n_chunks),
        out_shape=jax.ShapeDtypeStruct((batch, seq, z_w), BF16),
        scratch_shapes=[pltpu.VMEM((SEQS_PER_STEP, n_v, LANES, LANES), F32),
                        pltpu.VMEM((SEQS_PER_STEP, CARRY_ROWS, qkv_w), F32),
                        pltpu.VMEM((SEQS_PER_STEP, CHUNK, qkv_w), F32),
                        pltpu.VMEM((SEQS_PER_STEP, n_v, 4, CHUNK, LANES), F32),
                        pltpu.VMEM((SEQS_PER_STEP, LANES, LANES), F32)],
        compiler_params=pltpu.CompilerParams(dimension_semantics=("parallel", "arbitrary"),
                                             vmem_limit_bytes=VMEM_LIMIT_BYTES),
        name="gdn",
    )(proj_main, proj_main, proj_small, cw, al, dtb, nw)


def _pad_rows(w, rows):
    return jnp.concatenate([w, jnp.zeros((rows - w.shape[0], w.shape[1]), w.dtype)], axis=0)


def _even_params(j, width, ev_w_in, ev_mu, ev_w0, ev_w2, ev_a0, ev_a2, ev_k_k, ev_k_a, ev_r_k, ev_lnx_w, ev_lnx_b,
                 vr_w_down, vr_mu, vr_v0, vr_v2):
    d = ev_w_in.shape[1]
    w_in, mu = ev_w_in[j], ev_mu[j]
    rw = 4 * width
    lo_a, lo_b = rw + DECAY_RANK, rw + DECAY_RANK + ICL_RANK
    w_main = jnp.concatenate([w_in[:, :rw], w_in[:, lo_b:]], axis=1).astype(BF16)
    w_small = jnp.zeros((d, 3 * LANES), F32)
    w_small = w_small.at[:, 0:DECAY_RANK].set(w_in[:, rw:lo_a])
    w_small = w_small.at[:, LANES:LANES + ICL_RANK].set(w_in[:, lo_a:lo_b])
    mu_small = jnp.zeros((1, 3 * LANES), F32)
    mu_small = mu_small.at[0, 0:DECAY_RANK].set(mu[rw:lo_a])
    mu_small = mu_small.at[0, LANES:LANES + ICL_RANK].set(mu[lo_a:lo_b])
    row = lambda a: a.reshape(1, width)
    prm = dict(mu_main=mu[:rw].reshape(1, rw), w0=row(ev_w0[j]), w2=_pad_rows(ev_w2[j], LANES).astype(BF16),
               a0=row(ev_a0[j]), a2=_pad_rows(ev_a2[j], LANES).astype(BF16), k_k=row(ev_k_k[j]), k_a=row(ev_k_a[j]),
               r_k=row(ev_r_k[j]), lnx_w=row(ev_lnx_w[j]), lnx_b=row(ev_lnx_b[j]))
    if j > 0:
        w_small = w_small.at[:, 2 * LANES:2 * LANES + VRES_RANK].set(vr_w_down[j - 1])
        mu_small = mu_small.at[0, 2 * LANES:2 * LANES + VRES_RANK].set(vr_mu[j - 1])
        prm["v0"] = row(vr_v0[j - 1])
        prm["v2"] = _pad_rows(vr_v2[j - 1], LANES).astype(BF16)
    prm["mu_small"] = mu_small
    return w_main, w_small.astype(BF16), prm


def kernel(x, p, pre_g, post_g, ple_w_proj, ple_w_gate, ev_w_in, ev_mu, ev_w0, ev_w2, ev_a0, ev_a2, ev_k_k, ev_k_a, ev_r_k, ev_lnx_w, ev_lnx_b, vr_w_down, vr_mu, vr_v0, vr_v2, hg_lb, hg_norm_w, ev_w_out, od_w_in, od_conv_w, od_a_log, od_dt_bias, od_norm_w, od_w_out):
    batch, seq, d = x.shape
    assert batch % SEQS_PER_STEP == 0 and seq % CHUNK == 0
    depth = pre_g.shape[0]
    m = batch * seq
    n_chunks = seq // CHUNK
    width = ev_w0.shape[1]
    n_v = od_a_log.shape[1]
    n_qk = n_v // 2
    per_seq = lambda a: a.reshape(batch, seq, a.shape[-1])
    flat = lambda a: a.reshape(m, a.shape[-1])

    lb_all = jnp.cumsum(jax.nn.softmax(hg_lb.astype(F32), axis=0), axis=0)
    lb_all = lb_all - lb_all[0]

    xf = x.reshape(m, d)
    v_first = None
    for i in range(depth):
        j = i // 2
        if i % 2 == 0:
            w_main, w_small, prm = _even_params(j, width, ev_w_in, ev_mu, ev_w0, ev_w2, ev_a0, ev_a2, ev_k_k, ev_k_a,
                                                ev_r_k, ev_lnx_w, ev_lnx_b, vr_w_down, vr_mu, vr_v0, vr_v2)
            proj_main, proj_small = _norm_proj(xf, pre_g[i], w_main, w_small)
            proj_main, proj_small = per_seq(proj_main), per_seq(proj_small)
            if j == 0:
                mix_a, v_first = _rwkv(proj_main, proj_small, prm, None, n_chunks, width)
            else:
                mix_a = _rwkv(proj_main, proj_small, prm, v_first, n_chunks, width)
            mix_b = _hgrn(proj_main, 1, lb_all[j], hg_norm_w[j], j == 0, n_chunks, width)
            mixes, w_out = [flat(mix_a), flat(mix_b)], ev_w_out[j].astype(BF16)
        else:
            w_in = od_w_in[j]
            n_main = (4 * n_qk + n_v) * LANES
            w_small = jnp.zeros((d, LANES), F32).at[:, 0:2 * n_v].set(w_in[:, n_main:])
            proj_main, proj_small = _norm_proj(xf, pre_g[i], w_in[:, :n_main].astype(BF16), w_small.astype(BF16))
            mix = _gdn(per_seq(proj_main), per_seq(proj_small), od_conv_w[j], od_a_log[j], od_dt_bias[j],
                       od_norm_w[j], n_chunks, n_qk)
            mixes, w_out = [flat(mix)], od_w_out[j].astype(BF16)
        xf = _post(mixes, w_out, xf, post_g[i], p[i].reshape(m, -1), ple_w_gate[i].astype(BF16),
                   ple_w_proj[i].astype(BF16))
    return xf.reshape(batch, seq, d)
```

```python
import functools
import math

import jax
import jax.numpy as jnp
from jax import lax
from jax.experimental import pallas as pl
from jax.experimental.pallas import tpu as pltpu

F32 = jnp.float32
BF16 = jnp.bfloat16

LANES = 128
CARRY_ROWS = 8
VMEM_LIMIT_BYTES = 56 * 1024 * 1024

CHUNK = 64
SUB = 16
SEQS_PER_STEP = 2
HGRN_LOCKSTEP = 16
RWKV_LOCKSTEP = 16
GDN_LOCKSTEP = 8
NORM_EPS = 1e-6
RWKV_HEAD_DIM = 64
RWKV_GN_EPS = 64e-5
DECAY_RANK = 64
ICL_RANK = 64
VRES_RANK = 32
GDN_HEAD_DIM = 128
GDN_CONV = 4
LOG2E = math.log2(math.e)


def _dot(a, b):
    return jnp.dot(a, b, preferred_element_type=F32)


def _dot_nt(a, b):
    return lax.dot_general(a, b, (((1,), (1,)), ((), ())), preferred_element_type=F32)


def _dot_tn(a, b):
    return lax.dot_general(a, b, (((0,), (0,)), ((), ())), preferred_element_type=F32)


def _split3(x):
    hi = x.astype(BF16)
    r1 = x - hi.astype(F32)
    mid = r1.astype(BF16)
    lo = (r1 - mid.astype(F32)).astype(BF16)
    return hi, mid, lo


def _cumsum_rows(tril, x):
    n = x.shape[1]
    cs = _dot(tril, jnp.concatenate(_split3(x), axis=1))
    return (cs[:, 0:n] + cs[:, n:2 * n]) + cs[:, 2 * n:3 * n]


def _sigmoid(x):
    return 1.0 / (1.0 + jnp.exp(-x))


def _silu(x):
    return x * _sigmoid(x)


def _softplus(x):
    return jnp.maximum(x, 0.0) + jnp.log(1.0 + jnp.exp(-jnp.abs(x)))


def _eye(size):
    return (lax.broadcasted_iota(jnp.int32, (size, size), 0)
            == lax.broadcasted_iota(jnp.int32, (size, size), 1)).astype(F32)


def _neumann_inverse_steps(n):
    nb = n.astype(BF16)
    p = _dot(nb, nb)
    yield None
    x = _eye(n.shape[0]) + n
    for _ in range(int(math.log2(CHUNK)) - 2):
        pb = p.astype(BF16)
        xp = _dot(x.astype(BF16), pb)
        p2 = _dot(pb, pb)
        yield None
        x = x + xp
        p = p2
    xp = _dot(x.astype(BF16), p.astype(BF16))
    yield None
    yield (x + xp,)


def _run_lockstep(chain_makers, group):
    chain_makers = list(chain_makers)
    for start in range(0, len(chain_makers), group):
        chains = [make() for make in chain_makers[start:start + group]]
        while chains:
            alive = []
            for ch in chains:
                try:
                    next(ch)
                    alive.append(ch)
                except StopIteration:
                    pass
            chains = alive


def _lane_masks():
    lane = lax.broadcasted_iota(jnp.int32, (1, LANES), 1)
    m0 = (lane < LANES // 2).astype(F32)
    return lane, m0, 1.0 - m0


def _pair_time_masks():
    row = lax.broadcasted_iota(jnp.int32, (CHUNK, LANES), 0)
    src = lax.broadcasted_iota(jnp.int32, (CHUNK, LANES), 1) & (CHUNK - 1)
    return src < row, src <= row


def _tril_bf16():
    row = lax.broadcasted_iota(jnp.int32, (CHUNK, CHUNK), 0)
    col = lax.broadcasted_iota(jnp.int32, (CHUNK, CHUNK), 1)
    return (row >= col).astype(BF16)


def _seq_block(width, n_chunks, col_block=0):
    return pl.BlockSpec((SEQS_PER_STEP, CHUNK, width), lambda b, c: (b, c, col_block))


def _const_block(a):
    return pl.BlockSpec(a.shape, lambda b, c: (0,) * a.ndim)


def _rms_norm_bf16(x, g):
    ms = jnp.mean(x * x, axis=-1, keepdims=True)
    return (x * lax.rsqrt(ms + NORM_EPS) * g).astype(BF16)


def _norm_proj_kernel(x_ref, g_ref, wm_ref, ws_ref, om_ref, os_ref, h_ref, *, prenormed):
    src_ref = x_ref if prenormed else h_ref

    @pl.when(pl.program_id(1) == 0)
    def _():
        if not prenormed:
            h_ref[...] = _rms_norm_bf16(x_ref[...], g_ref[...])
        os_ref[...] = _dot(src_ref[...], ws_ref[...])

    om_ref[...] = _dot(src_ref[...], wm_ref[...])


def _norm_proj(x, g, wm, ws):
    m, d = x.shape
    nm, ns = wm.shape[1], ws.shape[1]
    tm = min(1024, m)
    tn = min(1024, nm)
    prenormed = g is None
    g = jnp.ones((1, d), F32) if prenormed else g.reshape(1, d)
    return pl.pallas_call(
        functools.partial(_norm_proj_kernel, prenormed=prenormed),
        grid=(m // tm, nm // tn),
        in_specs=[pl.BlockSpec((tm, d), lambda i, j: (i, 0)),
                  pl.BlockSpec((1, d), lambda i, j: (0, 0)),
                  pl.BlockSpec((d, tn), lambda i, j: (0, j)),
                  pl.BlockSpec((d, ns), lambda i, j: (0, 0))],
        out_specs=[pl.BlockSpec((tm, tn), lambda i, j: (i, j)),
                   pl.BlockSpec((tm, ns), lambda i, j: (i, 0))],
        out_shape=[jax.ShapeDtypeStruct((m, nm), F32), jax.ShapeDtypeStruct((m, ns), F32)],
        scratch_shapes=[pltpu.VMEM((tm, d), BF16)],
        compiler_params=pltpu.CompilerParams(dimension_semantics=("parallel", "arbitrary"),
                                             vmem_limit_bytes=VMEM_LIMIT_BYTES),
        name="norm_proj",
    )(x, g, wm, ws)


def _post_kernel(*refs, n_mix):
    mix_refs, w_refs = refs[:n_mix], refs[n_mix:2 * n_mix]
    x_ref, pg_ref, p_ref, wg_ref, wp_ref, ng_ref, o_ref, h_ref = refs[2 * n_mix:]
    y = _dot(mix_refs[0][...], w_refs[0][...])
    for mr, wr in zip(mix_refs[1:], w_refs[1:]):
        y = y + _dot(mr[...], wr[...])
    ms = jnp.mean(y * y, axis=-1, keepdims=True)
    x1 = x_ref[...] + y * lax.rsqrt(ms + NORM_EPS) * pg_ref[...]
    gate = _sigmoid(_dot(x1.astype(BF16), wg_ref[...]))
    x2 = x1 + gate * _dot(p_ref[...].astype(BF16), wp_ref[...])
    o_ref[...] = x2
    h_ref[...] = _rms_norm_bf16(x2, ng_ref[...])


def _post(mixes, w_out, x, post_g, p, w_gate, w_proj, next_g):
    m, d = x.shape
    tm = min(256, m)
    n_mix = len(mixes)
    k_mix = w_out.shape[0] // n_mix
    assert all(a.shape[1] == k_mix for a in mixes)
    row_spec = lambda a: pl.BlockSpec((tm, a.shape[1]), lambda i: (i, 0))
    full_spec = lambda a: pl.BlockSpec(a.shape, lambda i: (0, 0))
    w_specs = [pl.BlockSpec((k_mix, d), lambda i, blk=blk: (blk, 0)) for blk in range(n_mix)]
    w_outs = [w_out] * n_mix
    return pl.pallas_call(
        functools.partial(_post_kernel, n_mix=n_mix),
        grid=(m // tm,),
        in_specs=([row_spec(a) for a in mixes] + w_specs
                  + [row_spec(x), pl.BlockSpec((1, d), lambda i: (0, 0)), row_spec(p),
                     full_spec(w_gate), full_spec(w_proj), pl.BlockSpec((1, d), lambda i: (0, 0))]),
        out_specs=[row_spec(x), row_spec(x)],
        out_shape=[jax.ShapeDtypeStruct((m, d), F32), jax.ShapeDtypeStruct((m, d), BF16)],
        compiler_params=pltpu.CompilerParams(dimension_semantics=("parallel",),
                                             vmem_limit_bytes=VMEM_LIMIT_BYTES),
        name="post",
    )(*mixes, *w_outs, x, post_g.reshape(1, d), p, w_gate, w_proj, next_g.reshape(1, d))


def _hgrn_kernel(pm_ref, lb_ref, nw_ref, sel_ref, out_ref, s_ref, kc_ref, *, first, n_heads, width):
    L = CHUNK

    @pl.when(pl.program_id(1) == 0)
    def _():
        s_ref[...] = jnp.zeros_like(s_ref)

    row = lax.broadcasted_iota(jnp.int32, (L, L), 0)
    col = lax.broadcasted_iota(jnp.int32, (L, L), 1)
    tril = _tril_bf16()
    diag_mask = jnp.logical_and(col // SUB == row // SUB, col <= row)
    nw = nw_ref[...]
    sel = sel_ref[...]

    def head_chain(sq, h):
        ls = slice(h * LANES, (h + 1) * LANES)
        qp = pm_ref[sq, :, h * LANES:(h + 1) * LANES]
        z = pm_ref[sq, :, width + h * LANES:width + (h + 1) * LANES]
        v = pm_ref[sq, :, 2 * width + h * LANES:2 * width + (h + 1) * LANES].astype(BF16)

        e = jnp.exp(-jnp.abs(z))
        inv = 1.0 / (1.0 + e)
        pos = z >= 0.0
        sig = jnp.where(pos, inv, e * inv)
        nsig = jnp.where(pos, e * inv, inv)
        if first:
            lf = jnp.minimum(z, 0.0) - jnp.log(1.0 + e)
            k = nsig
        else:
            lb = lb_ref[:, ls]
            lf = jnp.log(lb + (1.0 - lb) * sig)
            k = (1.0 - lb) * nsig
        q = _silu(qp)
        c = _cumsum_rows(tril, lf)
        yield
        c = c * LOG2E
        kc_ref[sq, h, 0] = k
        kc_ref[sq, h, 1] = c

        att_rows = [jnp.zeros((SUB, L), F32)]
        for i in range(1, L // SUB):
            ref_c = kc_ref[sq, h, 1, i * SUB:i * SUB + 1, :]
            kt = (k[0:i * SUB] * jnp.exp2(ref_c - c[0:i * SUB])).astype(BF16)
            kt = jnp.concatenate([kt, jnp.zeros((L - i * SUB, LANES), BF16)], axis=0)
            qt = (q[i * SUB:(i + 1) * SUB] * jnp.exp2(c[i * SUB:(i + 1) * SUB] - ref_c)).astype(BF16)
            att_rows.append(_dot_nt(qt, kt))

        p_rows = []
        for i in range(L // SUB):
            q_i = q[i * SUB:(i + 1) * SUB]
            c_i = c[i * SUB:(i + 1) * SUB]
            pieces = []
            for j in range(SUB):
                k_row = kc_ref[sq, h, 0, i * SUB + j:i * SUB + j + 1, :]
                c_row = kc_ref[sq, h, 1, i * SUB + j:i * SUB + j + 1, :]
                dec = jnp.exp2(jnp.minimum(c_i - c_row, 0.0))
                pieces.append((q_i * dec * k_row).astype(BF16))
            p_rows.append(jnp.concatenate(pieces, axis=1))
        att_diag = _dot(jnp.concatenate(p_rows, axis=0), sel)

        st = s_ref[sq, h]
        qs = _dot_nt((q * jnp.exp2(c)).astype(BF16), st.astype(BF16))
        c_last = kc_ref[sq, h, 1, L - 1:L, :]
        upd = _dot_tn(v, (k * jnp.exp2(c_last - c)).astype(BF16))
        yield
        att = jnp.concatenate(att_rows, axis=0) + jnp.where(diag_mask, att_diag[:, :L], 0.0)
        o = qs + _dot(att.astype(BF16), v)
        s_ref[sq, h] = jnp.exp2(c_last) * st + upd
        yield
        gb = pm_ref[sq, :, 3 * width + h * LANES:3 * width + (h + 1) * LANES]
        ms = jnp.mean(o * o, axis=-1, keepdims=True)
        out_ref[sq, :, ls] = (o * lax.rsqrt(ms + NORM_EPS) * nw * _silu(gb)).astype(BF16)

    _run_lockstep([functools.partial(head_chain, sq, h) for sq in range(SEQS_PER_STEP) for h in range(n_heads)],
                  HGRN_LOCKSTEP)


def _hgrn(proj_main, col_block, lb, norm_w, first, n_chunks, width):
    batch, seq, _ = proj_main.shape
    n_heads = width // LANES
    j = jnp.arange(SUB * LANES) // LANES
    s = jnp.arange(LANES)
    sel = ((s[None, :] % SUB == j[:, None]) & (s[None, :] < CHUNK)).astype(BF16)
    args = [proj_main, lb.reshape(1, width), norm_w.reshape(1, LANES), sel]
    return pl.pallas_call(
        functools.partial(_hgrn_kernel, first=first, n_heads=n_heads, width=width),
        grid=(batch // SEQS_PER_STEP, n_chunks),
        in_specs=[_seq_block(4 * width, n_chunks, col_block)] + [_const_block(a) for a in args[1:]],
        out_specs=_seq_block(width, n_chunks),
        out_shape=jax.ShapeDtypeStruct((batch, seq, width), BF16),
        scratch_shapes=[pltpu.VMEM((SEQS_PER_STEP, n_heads, LANES, LANES), F32),
                        pltpu.VMEM((SEQS_PER_STEP, n_heads, 2, CHUNK, LANES), F32)],
        compiler_params=pltpu.CompilerParams(dimension_semantics=("parallel", "arbitrary"),
                                             vmem_limit_bytes=VMEM_LIMIT_BYTES),
        name="hgrn2",
    )(*args)


def _rwkv_kernel(*refs, has_vres, n_pairs, width):
    (pm_ref, ps_ref, mum_ref, mus_ref, w0_ref, w2_ref, a0_ref, a2_ref, kk_ref, ka_ref, rk_ref,
     lnw_ref, lnb_ref) = refs[:13]
    if has_vres:
        vf_ref, v0_ref, v2_ref, out_ref, s_ref, cm_ref, cs_ref, fm_ref = refs[13:]
        vfo_ref = None
    else:
        out_ref, vfo_ref, s_ref, cm_ref, cs_ref, fm_ref = refs[13:]
    L = CHUNK

    @pl.when(pl.program_id(1) == 0)
    def _():
        s_ref[...] = jnp.zeros_like(s_ref)
        cm_ref[...] = jnp.zeros_like(cm_ref)
        cs_ref[...] = jnp.zeros_like(cs_ref)

    row1 = lax.broadcasted_iota(jnp.int32, (L, 1), 0)

    def token_shift(x, carry_row, mu):
        prev = jnp.where(row1 == 0, carry_row, pltpu.roll(x, 1, 0))
        return x + (prev - x) * mu

    tril = _tril_bf16()

    for sq in range(SEQS_PER_STEP):
        ps = ps_ref[sq]
        pss = token_shift(ps, cs_ref[sq, 0:1, :], mus_ref[...])
        cs_ref[sq, 0:1, :] = ps[L - 1:L, :]
        wd_t = jnp.tanh(pss[:, 0:LANES]).astype(BF16)
        ad = pss[:, LANES:2 * LANES].astype(BF16)
        lw_all = -math.exp(-0.5) * _sigmoid(w0_ref[...] + _dot(wd_t, w2_ref[...]))
        fm_ref[sq, 0] = lw_all * LOG2E
        fm_ref[sq, 1] = _cumsum_rows(tril, lw_all) * LOG2E
        fm_ref[sq, 2] = _sigmoid(a0_ref[...] + _dot(ad, a2_ref[...]))
        if has_vres:
            vd = pss[:, 2 * LANES:3 * LANES].astype(BF16)
            fm_ref[sq, 3] = _sigmoid(v0_ref[...] + _dot(vd, v2_ref[...]))

    _, m0, m1 = _lane_masks()
    strict2, incl2 = _pair_time_masks()
    rr = lax.broadcasted_iota(jnp.int32, (LANES, LANES), 0) // RWKV_HEAD_DIM
    cc = lax.broadcasted_iota(jnp.int32, (LANES, LANES), 1) // RWKV_HEAD_DIM
    bd_mask = (rr == cc).astype(F32)
    g128 = bd_mask.astype(BF16)
    inv_hd = 1.0 / RWKV_HEAD_DIM

    def stack_heads(x):
        return jnp.concatenate([x * m0, x * m1], axis=0)

    def pair_chain(sq, j):
        ls = slice(j * LANES, (j + 1) * LANES)

        def load(q):
            sl = slice(q * width + j * LANES, q * width + (j + 1) * LANES)
            x = pm_ref[sq, :, sl]
            y = token_shift(x, cm_ref[sq, 0:1, sl], mum_ref[:, sl])
            cm_ref[sq, 0:1, sl] = x[L - 1:L, :]
            return y

        r, k, v = load(0), load(1), load(2)
        lw, c, icl = fm_ref[sq, 0, :, ls], fm_ref[sq, 1, :, ls], fm_ref[sq, 2, :, ls]
        if has_vres:
            v = v + (vf_ref[sq, :, ls] - v) * fm_ref[sq, 3, :, ls]
        else:
            vfo_ref[sq, :, ls] = v
        kkr = k * kk_ref[:, ls]
        k2 = k * (1.0 + (icl - 1.0) * ka_ref[:, ls])
        sums = _dot(jnp.concatenate([kkr * kkr, r * k2 * rk_ref[:, ls]], axis=0).astype(BF16), g128)
        yield
        kkn = kkr * lax.rsqrt(sums[0:L] + 1e-12)
        bonus = sums[L:2 * L] * v
        b = kkn * icl
        c_last = c[L - 1:L, :]
        e_neg = jnp.exp2(-c)
        e_tail = jnp.exp2(c_last - c)
        a_t = -kkn * jnp.exp2(c - lw)
        r_t = r * jnp.exp2(c)
        lhs = jnp.concatenate([a_t, r_t], axis=0).astype(BF16)
        rhs_t = jnp.concatenate([stack_heads(b * e_neg), stack_heads(k2 * e_neg)], axis=0).astype(BF16)
        mm = _dot_nt(lhs, rhs_t)
        st = s_ref[sq, j]
        sr = _dot_nt(lhs, st.astype(BF16))
        yield
        m_ab = jnp.where(strict2, mm[0:L, 0:LANES], 0.0)
        m_ak = jnp.where(strict2, mm[0:L, LANES:2 * LANES], 0.0)
        p_rb = jnp.where(incl2, mm[L:2 * L, 0:LANES], 0.0)
        p_rk = jnp.where(incl2, mm[L:2 * L, LANES:2 * LANES], 0.0)
        v_bd = stack_heads(v).astype(BF16)
        rhs = sr[0:L] + _dot(m_ak.astype(BF16), v_bd)
        for step in _neumann_inverse_steps(stack_heads(m_ab)):
            if step is None:
                yield
            else:
                t_bd, = step
        t_cat = (t_bd[0:L] + t_bd[L:2 * L]).astype(BF16)
        u = _dot(t_cat, stack_heads(rhs).astype(BF16))
        yield
        u_bd = stack_heads(u).astype(BF16)
        o = sr[L:2 * L] + _dot(jnp.concatenate([p_rb, p_rk], axis=1).astype(BF16),
                               jnp.concatenate([u_bd, v_bd], axis=0))
        upd = _dot_tn(jnp.concatenate([u, v], axis=0).astype(BF16),
                      jnp.concatenate([b * e_tail, k2 * e_tail], axis=0).astype(BF16))
        yield
        s_ref[sq, j] = (jnp.exp2(c_last) * st + upd) * bd_mask
        mean = _dot(o.astype(BF16), g128) * inv_hd
        yield
        dlt = o - mean
        var = _dot((dlt * dlt).astype(BF16), g128) * inv_hd
        yield
        g = load(3)
        o_n = dlt * lax.rsqrt(var + RWKV_GN_EPS) * lnw_ref[:, ls] + lnb_ref[:, ls]
        out_ref[sq, :, ls] = ((o_n + bonus) * _silu(g)).astype(BF16)

    _run_lockstep([functools.partial(pair_chain, sq, j) for sq in range(SEQS_PER_STEP) for j in range(n_pairs)],
                  RWKV_LOCKSTEP)


def _rwkv(proj_main, proj_small, prm, v_first, n_chunks, width):
    batch, seq, _ = proj_main.shape
    n_pairs = width // LANES
    has_vres = v_first is not None
    ns = proj_small.shape[2]
    consts = [prm["mu_main"], prm["mu_small"], prm["w0"], prm["w2"], prm["a0"], prm["a2"],
              prm["k_k"], prm["k_a"], prm["r_k"], prm["lnx_w"], prm["lnx_b"]]
    args = [proj_main, proj_small] + consts
    in_specs = [_seq_block(4 * width, n_chunks), _seq_block(ns, n_chunks)] + [_const_block(a) for a in consts]
    if has_vres:
        args += [v_first, prm["v0"], prm["v2"]]
        in_specs += [_seq_block(width, n_chunks), _const_block(prm["v0"]), _const_block(prm["v2"])]
        out_specs = _seq_block(width, n_chunks)
        out_shape = jax.ShapeDtypeStruct((batch, seq, width), BF16)
    else:
        out_specs = [_seq_block(width, n_chunks), _seq_block(width, n_chunks)]
        out_shape = [jax.ShapeDtypeStruct((batch, seq, width), BF16), jax.ShapeDtypeStruct((batch, seq, width), F32)]
    return pl.pallas_call(
        functools.partial(_rwkv_kernel, has_vres=has_vres, n_pairs=n_pairs, width=width),
        grid=(batch // SEQS_PER_STEP, n_chunks),
        in_specs=in_specs,
        out_specs=out_specs,
        out_shape=out_shape,
        scratch_shapes=[pltpu.VMEM((SEQS_PER_STEP, n_pairs, LANES, LANES), F32),
                        pltpu.VMEM((SEQS_PER_STEP, CARRY_ROWS, 4 * width), F32),
                        pltpu.VMEM((SEQS_PER_STEP, CARRY_ROWS, ns), F32),
                        pltpu.VMEM((SEQS_PER_STEP, 4, CHUNK, width), F32)],
        compiler_params=pltpu.CompilerParams(dimension_semantics=("parallel", "arbitrary"),
                                             vmem_limit_bytes=VMEM_LIMIT_BYTES),
        name="rwkv7",
    )(*args)


def _gdn_kernel(qkv_ref, z_ref, sm_ref, cw_ref, al_ref, dtb_ref, nw_ref, out_ref,
                s_ref, carry_ref, act_ref, gate_ref, grow_ref, *, n_qk):
    L = CHUNK
    n_v = 2 * n_qk

    @pl.when(pl.program_id(1) == 0)
    def _():
        s_ref[...] = jnp.zeros_like(s_ref)
        carry_ref[...] = jnp.zeros_like(carry_ref)

    q_scale = GDN_HEAD_DIM ** -0.5
    tril = _tril_bf16()

    for sq in range(SEQS_PER_STEP):
        for gi in range(4 * n_qk):
            sl = slice(gi * LANES, (gi + 1) * LANES)
            x = qkv_ref[sq, :, sl]
            ext = jnp.concatenate([carry_ref[sq, :, sl], x], axis=0)
            acc = x * cw_ref[GDN_CONV - 1:GDN_CONV, sl]
            for d in range(1, GDN_CONV):
                acc = acc + pltpu.roll(ext, d, 0)[CARRY_ROWS:] * cw_ref[GDN_CONV - 1 - d:GDN_CONV - d, sl]
            carry_ref[sq, :, sl] = x[L - CARRY_ROWS:L, :]
            y = _silu(acc)
            if gi < 2 * n_qk:
                y = y * lax.rsqrt(jnp.sum(y * y, axis=-1, keepdims=True) + 1e-6)
                if gi < n_qk:
                    y = y * q_scale
            act_ref[sq, :, sl] = y

        sm = sm_ref[sq]
        beta = _sigmoid(sm)
        g_log = -jnp.exp(al_ref[...]) * _softplus(sm + dtb_ref[...])
        g2 = _cumsum_rows(tril, g_log) * LOG2E
        eg = jnp.exp2(g2)
        beg = beta * pltpu.roll(eg, LANES - n_v, 1)
        grow_ref[sq] = jnp.concatenate([g2, jnp.zeros((LANES - L, LANES), F32)], axis=0).T
        for h in range(n_v):
            gate_ref[sq, h, 0] = jnp.broadcast_to(g2[:, n_v + h:n_v + h + 1], (L, LANES))
            gate_ref[sq, h, 1] = jnp.broadcast_to(beta[:, h:h + 1], (L, LANES))
            gate_ref[sq, h, 2] = jnp.broadcast_to(eg[:, n_v + h:n_v + h + 1], (L, LANES))
            gate_ref[sq, h, 3] = jnp.broadcast_to(beg[:, h:h + 1], (L, LANES))

    lane, m0, m1 = _lane_masks()
    low = lane < LANES // 2
    strict2, incl2 = _pair_time_masks()
    nw = nw_ref[...]

    def pair_chain(sq, p):
        q_h = act_ref[sq, :, p * LANES:(p + 1) * LANES]
        k_h = act_ref[sq, :, (n_qk + p) * LANES:(n_qk + p + 1) * LANES]
        q_b, k_b = q_h.astype(BF16), k_h.astype(BF16)
        heads = (2 * p, 2 * p + 1)
        mm = _dot_nt(jnp.concatenate([q_b, k_b], axis=0), jnp.concatenate([k_b, k_b], axis=0))
        states = [s_ref[sq, h] for h in heads]
        states_b = [s.astype(BF16) for s in states]
        ks = [_dot((gate_ref[sq, h, 3] * k_h).astype(BF16), states_b[e]) for e, h in enumerate(heads)]
        yield
        g_full = [gate_ref[sq, h, 0] for h in heads]
        b_full = [gate_ref[sq, h, 1] for h in heads]
        g_cat = jnp.where(low, g_full[0], g_full[1])
        b_cat = jnp.where(low, b_full[0], b_full[1])
        g_row = (grow_ref[sq, n_v + heads[0]:n_v + heads[0] + 1, :]
                 + pltpu.roll(grow_ref[sq, n_v + heads[1]:n_v + heads[1] + 1, :], LANES // 2, 1))
        dm = jnp.where(incl2, jnp.exp2(jnp.minimum(g_cat - g_row, 0.0)), 0.0)
        n_cat = jnp.where(strict2, -(b_cat * mm[L:2 * L] * dm), 0.0)
        qk_dm = mm[0:L] * dm
        rhs = []
        for e, h in enumerate(heads):
            v_h = act_ref[sq, :, (2 * n_qk + h) * LANES:(2 * n_qk + h + 1) * LANES]
            rhs.append(b_full[e] * v_h - ks[e])
        rhs = jnp.concatenate(rhs, axis=0).astype(BF16)
        for step in _neumann_inverse_steps(jnp.concatenate([n_cat * m0, n_cat * m1], axis=0)):
            if step is None:
                yield
            else:
                t_bd, = step
        new_v = _dot(t_bd.astype(BF16), rhs)
        yield
        nv_stack = new_v.astype(BF16)
        outs, upds, g_lasts = [], [], []
        for e, h in enumerate(heads):
            mask = m0 if e == 0 else m1
            lhs = jnp.concatenate([q_h * gate_ref[sq, h, 2], qk_dm * mask], axis=1).astype(BF16)
            outs.append(_dot(lhs, jnp.concatenate([states_b[e], nv_stack], axis=0)))
            g_lasts.append(g_full[e][L - 1:L, :])
            k_tail = (k_h * jnp.exp2(g_lasts[e] - g_full[e])).astype(BF16)
            upds.append(_dot_tn(k_tail, nv_stack[e * L:(e + 1) * L]))
        yield
        for e, h in enumerate(heads):
            s_ref[sq, h] = jnp.exp2(g_lasts[e]) * states[e] + upds[e]
            o = outs[e]
            z = z_ref[sq, :, h * LANES:(h + 1) * LANES]
            ms = jnp.mean(o * o, axis=-1, keepdims=True)
            out_ref[sq, :, h * LANES:(h + 1) * LANES] = (o * lax.rsqrt(ms + NORM_EPS) * nw * _silu(z)).astype(BF16)

    _run_lockstep([functools.partial(pair_chain, sq, p) for sq in range(SEQS_PER_STEP) for p in range(n_qk)],
                  GDN_LOCKSTEP)


def _gdn(proj_main, proj_small, conv_w, a_log, dt_bias, norm_w, n_chunks, n_qk):
    batch, seq, _ = proj_main.shape
    n_v = 2 * n_qk
    qkv_w = 4 * n_qk * LANES
    z_w = n_v * LANES
    cw = jnp.concatenate([conv_w, jnp.zeros((CARRY_ROWS - GDN_CONV, qkv_w), F32)], axis=0)
    al = jnp.zeros((1, LANES), F32).at[0, n_v:2 * n_v].set(a_log)
    dtb = jnp.zeros((1, LANES), F32).at[0, n_v:2 * n_v].set(dt_bias)
    nw = norm_w.reshape(1, LANES)
    return pl.pallas_call(
        functools.partial(_gdn_kernel, n_qk=n_qk),
        grid=(batch // SEQS_PER_STEP, n_chunks),
        in_specs=[_seq_block(qkv_w, n_chunks), _seq_block(z_w, n_chunks, qkv_w // z_w), _seq_block(LANES, n_chunks),
                  _const_block(cw), _const_block(al), _const_block(dtb), _const_block(nw)],
        out_specs=_seq_block(z_w, n_chunks),
        out_shape=jax.ShapeDtypeStruct((batch, seq, z_w), BF16),
        scratch_shapes=[pltpu.VMEM((SEQS_PER_STEP, n_v, LANES, LANES), F32),
                        pltpu.VMEM((SEQS_PER_STEP, CARRY_ROWS, qkv_w), F32),
                        pltpu.VMEM((SEQS_PER_STEP, CHUNK, qkv_w), F32),
                        pltpu.VMEM((SEQS_PER_STEP, n_v, 4, CHUNK, LANES), F32),
                        pltpu.VMEM((SEQS_PER_STEP, LANES, LANES), F32)],
        compiler_params=pltpu.CompilerParams(dimension_semantics=("parallel", "arbitrary"),
                                             vmem_limit_bytes=VMEM_LIMIT_BYTES),
        name="gdn",
    )(proj_main, proj_main, proj_small, cw, al, dtb, nw)


def _pad_rows(w, rows):
    return jnp.concatenate([w, jnp.zeros((rows - w.shape[0], w.shape[1]), w.dtype)], axis=0)


def _even_params(j, width, ev_w_in, ev_mu, ev_w0, ev_w2, ev_a0, ev_a2, ev_k_k, ev_k_a, ev_r_k, ev_lnx_w, ev_lnx_b,
                 vr_w_down, vr_mu, vr_v0, vr_v2):
    d = ev_w_in.shape[1]
    w_in, mu = ev_w_in[j], ev_mu[j]
    rw = 4 * width
    lo_a, lo_b = rw + DECAY_RANK, rw + DECAY_RANK + ICL_RANK
    w_main = jnp.concatenate([w_in[:, :rw], w_in[:, lo_b:]], axis=1).astype(BF16)
    w_small = jnp.zeros((d, 3 * LANES), F32)
    w_small = w_small.at[:, 0:DECAY_RANK].set(w_in[:, rw:lo_a])
    w_small = w_small.at[:, LANES:LANES + ICL_RANK].set(w_in[:, lo_a:lo_b])
    mu_small = jnp.zeros((1, 3 * LANES), F32)
    mu_small = mu_small.at[0, 0:DECAY_RANK].set(mu[rw:lo_a])
    mu_small = mu_small.at[0, LANES:LANES + ICL_RANK].set(mu[lo_a:lo_b])
    row = lambda a: a.reshape(1, width)
    prm = dict(mu_main=mu[:rw].reshape(1, rw), w0=row(ev_w0[j]), w2=_pad_rows(ev_w2[j], LANES).astype(BF16),
               a0=row(ev_a0[j]), a2=_pad_rows(ev_a2[j], LANES).astype(BF16), k_k=row(ev_k_k[j]), k_a=row(ev_k_a[j]),
               r_k=row(ev_r_k[j]), lnx_w=row(ev_lnx_w[j]), lnx_b=row(ev_lnx_b[j]))
    if j > 0:
        w_small = w_small.at[:, 2 * LANES:2 * LANES + VRES_RANK].set(vr_w_down[j - 1])
        mu_small = mu_small.at[0, 2 * LANES:2 * LANES + VRES_RANK].set(vr_mu[j - 1])
        prm["v0"] = row(vr_v0[j - 1])
        prm["v2"] = _pad_rows(vr_v2[j - 1], LANES).astype(BF16)
    prm["mu_small"] = mu_small
    return w_main, w_small.astype(BF16), prm


def kernel(x, p, pre_g, post_g, ple_w_proj, ple_w_gate, ev_w_in, ev_mu, ev_w0, ev_w2, ev_a0, ev_a2, ev_k_k, ev_k_a, ev_r_k, ev_lnx_w, ev_lnx_b, vr_w_down, vr_mu, vr_v0, vr_v2, hg_lb, hg_norm_w, ev_w_out, od_w_in, od_conv_w, od_a_log, od_dt_bias, od_norm_w, od_w_out):
    batch, seq, d = x.shape
    assert batch % SEQS_PER_STEP == 0 and seq % CHUNK == 0
    depth = pre_g.shape[0]
    m = batch * seq
    n_chunks = seq // CHUNK
    width = ev_w0.shape[1]
    n_v = od_a_log.shape[1]
    n_qk = n_v // 2
    per_seq = lambda a: a.reshape(batch, seq, a.shape[-1])
    flat = lambda a: a.reshape(m, a.shape[-1])

    lb_all = jnp.cumsum(jax.nn.softmax(hg_lb.astype(F32), axis=0), axis=0)
    lb_all = lb_all - lb_all[0]

    xf = x.reshape(m, d)
    v_first = None
    h_in, g_in = xf, pre_g[0]
    for i in range(depth):
        j = i // 2
        if i % 2 == 0:
            w_main, w_small, prm = _even_params(j, width, ev_w_in, ev_mu, ev_w0, ev_w2, ev_a0, ev_a2, ev_k_k, ev_k_a,
                                                ev_r_k, ev_lnx_w, ev_lnx_b, vr_w_down, vr_mu, vr_v0, vr_v2)
            proj_main, proj_small = _norm_proj(h_in, g_in, w_main, w_small)
            proj_main, proj_small = per_seq(proj_main), per_seq(proj_small)
            if j == 0:
                mix_a, v_first = _rwkv(proj_main, proj_small, prm, None, n_chunks, width)
            else:
                mix_a = _rwkv(proj_main, proj_small, prm, v_first, n_chunks, width)
            mix_b = _hgrn(proj_main, 1, lb_all[j], hg_norm_w[j], j == 0, n_chunks, width)
            mixes, w_out = [flat(mix_a), flat(mix_b)], ev_w_out[j].astype(BF16)
        else:
            w_in = od_w_in[j]
            n_main = (4 * n_qk + n_v) * LANES
            w_small = jnp.zeros((d, LANES), F32).at[:, 0:2 * n_v].set(w_in[:, n_main:])
            proj_main, proj_small = _norm_proj(h_in, g_in, w_in[:, :n_main].astype(BF16), w_small.astype(BF16))
            mix = _gdn(per_seq(proj_main), per_seq(proj_small), od_conv_w[j], od_a_log[j], od_dt_bias[j],
                       od_norm_w[j], n_chunks, n_qk)
            mixes, w_out = [flat(mix)], od_w_out[j].astype(BF16)
        xf, h_in = _post(mixes, w_out, xf, post_g[i], p[i].reshape(m, -1), ple_w_gate[i].astype(BF16),
                         ple_w_proj[i].astype(BF16), pre_g[min(i + 1, depth - 1)])
        g_in = None
    return xf.reshape(batch, seq, d)
```

```python
import functools
import math

import jax
import jax.numpy as jnp
from jax import lax
from jax.experimental import pallas as pl
from jax.experimental.pallas import tpu as pltpu

F32 = jnp.float32
BF16 = jnp.bfloat16

LANES = 128
CARRY_ROWS = 8
VMEM_LIMIT_BYTES = 56 * 1024 * 1024

CHUNK = 64
SUB = 8
BF16_ROWS = 16
SEQS_PER_STEP = 2
HGRN_LOCKSTEP = 16
RWKV_LOCKSTEP = 16
GDN_LOCKSTEP = 8
NORM_EPS = 1e-6
RWKV_HEAD_DIM = 64
RWKV_GN_EPS = 64e-5
DECAY_RANK = 64
ICL_RANK = 64
VRES_RANK = 32
GDN_HEAD_DIM = 128
GDN_CONV = 4
LOG2E = math.log2(math.e)


def _dot(a, b):
    return jnp.dot(a, b, preferred_element_type=F32)


def _dot_nt(a, b):
    return lax.dot_general(a, b, (((1,), (1,)), ((), ())), preferred_element_type=F32)


def _dot_tn(a, b):
    return lax.dot_general(a, b, (((0,), (0,)), ((), ())), preferred_element_type=F32)


def _split3(x):
    hi = x.astype(BF16)
    r1 = x - hi.astype(F32)
    mid = r1.astype(BF16)
    lo = (r1 - mid.astype(F32)).astype(BF16)
    return hi, mid, lo


def _cumsum_rows(tril, x):
    n = x.shape[1]
    cs = _dot(tril, jnp.concatenate(_split3(x), axis=1))
    return (cs[:, 0:n] + cs[:, n:2 * n]) + cs[:, 2 * n:3 * n]


def _sigmoid(x):
    return 1.0 / (1.0 + jnp.exp(-x))


def _silu(x):
    return x * _sigmoid(x)


def _softplus(x):
    return jnp.maximum(x, 0.0) + jnp.log(1.0 + jnp.exp(-jnp.abs(x)))


def _eye(size):
    return (lax.broadcasted_iota(jnp.int32, (size, size), 0)
            == lax.broadcasted_iota(jnp.int32, (size, size), 1)).astype(F32)


def _neumann_inverse_steps(n):
    nb = n.astype(BF16)
    p = _dot(nb, nb)
    yield None
    x = _eye(n.shape[0]) + n
    for _ in range(int(math.log2(CHUNK)) - 2):
        pb = p.astype(BF16)
        xp = _dot(x.astype(BF16), pb)
        p2 = _dot(pb, pb)
        yield None
        x = x + xp
        p = p2
    xp = _dot(x.astype(BF16), p.astype(BF16))
    yield None
    yield (x + xp,)


def _run_lockstep(chain_makers, group):
    chain_makers = list(chain_makers)
    for start in range(0, len(chain_makers), group):
        chains = [make() for make in chain_makers[start:start + group]]
        while chains:
            alive = []
            for ch in chains:
                try:
                    next(ch)
                    alive.append(ch)
                except StopIteration:
                    pass
            chains = alive


def _lane_masks():
    lane = lax.broadcasted_iota(jnp.int32, (1, LANES), 1)
    m0 = (lane < LANES // 2).astype(F32)
    return lane, m0, 1.0 - m0


def _pair_time_masks():
    row = lax.broadcasted_iota(jnp.int32, (CHUNK, LANES), 0)
    src = lax.broadcasted_iota(jnp.int32, (CHUNK, LANES), 1) & (CHUNK - 1)
    return src < row, src <= row


def _tril_bf16():
    row = lax.broadcasted_iota(jnp.int32, (CHUNK, CHUNK), 0)
    col = lax.broadcasted_iota(jnp.int32, (CHUNK, CHUNK), 1)
    return (row >= col).astype(BF16)


def _seq_block(width, n_chunks, col_block=0):
    return pl.BlockSpec((SEQS_PER_STEP, CHUNK, width), lambda b, c: (b, c, col_block))


def _const_block(a):
    return pl.BlockSpec(a.shape, lambda b, c: (0,) * a.ndim)


def _rms_norm_bf16(x, g):
    ms = jnp.mean(x * x, axis=-1, keepdims=True)
    return (x * lax.rsqrt(ms + NORM_EPS) * g).astype(BF16)


def _norm_proj_kernel(x_ref, g_ref, wm_ref, ws_ref, om_ref, os_ref, h_ref, *, prenormed):
    src_ref = x_ref if prenormed else h_ref

    @pl.when(pl.program_id(1) == 0)
    def _():
        if not prenormed:
            h_ref[...] = _rms_norm_bf16(x_ref[...], g_ref[...])
        os_ref[...] = _dot(src_ref[...], ws_ref[...])

    om_ref[...] = _dot(src_ref[...], wm_ref[...])


def _norm_proj(x, g, wm, ws):
    m, d = x.shape
    nm, ns = wm.shape[1], ws.shape[1]
    tm = min(1024, m)
    tn = min(1024, nm)
    prenormed = g is None
    g = jnp.ones((1, d), F32) if prenormed else g.reshape(1, d)
    return pl.pallas_call(
        functools.partial(_norm_proj_kernel, prenormed=prenormed),
        grid=(m // tm, nm // tn),
        in_specs=[pl.BlockSpec((tm, d), lambda i, j: (i, 0)),
                  pl.BlockSpec((1, d), lambda i, j: (0, 0)),
                  pl.BlockSpec((d, tn), lambda i, j: (0, j)),
                  pl.BlockSpec((d, ns), lambda i, j: (0, 0))],
        out_specs=[pl.BlockSpec((tm, tn), lambda i, j: (i, j)),
                   pl.BlockSpec((tm, ns), lambda i, j: (i, 0))],
        out_shape=[jax.ShapeDtypeStruct((m, nm), F32), jax.ShapeDtypeStruct((m, ns), F32)],
        scratch_shapes=[pltpu.VMEM((tm, d), BF16)],
        compiler_params=pltpu.CompilerParams(dimension_semantics=("parallel", "arbitrary"),
                                             vmem_limit_bytes=VMEM_LIMIT_BYTES),
        name="norm_proj",
    )(x, g, wm, ws)


def _post_kernel(*refs, n_mix):
    mix_refs, w_refs = refs[:n_mix], refs[n_mix:2 * n_mix]
    x_ref, pg_ref, p_ref, wg_ref, wp_ref, ng_ref, o_ref, h_ref = refs[2 * n_mix:]
    y = _dot(mix_refs[0][...], w_refs[0][...])
    for mr, wr in zip(mix_refs[1:], w_refs[1:]):
        y = y + _dot(mr[...], wr[...])
    ms = jnp.mean(y * y, axis=-1, keepdims=True)
    x1 = x_ref[...] + y * lax.rsqrt(ms + NORM_EPS) * pg_ref[...]
    gate = _sigmoid(_dot(x1.astype(BF16), wg_ref[...]))
    x2 = x1 + gate * _dot(p_ref[...].astype(BF16), wp_ref[...])
    o_ref[...] = x2
    h_ref[...] = _rms_norm_bf16(x2, ng_ref[...])


def _post(mixes, w_out, x, post_g, p, w_gate, w_proj, next_g):
    m, d = x.shape
    tm = min(256, m)
    n_mix = len(mixes)
    k_mix = w_out.shape[0] // n_mix
    assert all(a.shape[1] == k_mix for a in mixes)
    row_spec = lambda a: pl.BlockSpec((tm, a.shape[1]), lambda i: (i, 0))
    full_spec = lambda a: pl.BlockSpec(a.shape, lambda i: (0, 0))
    w_specs = [pl.BlockSpec((k_mix, d), lambda i, blk=blk: (blk, 0)) for blk in range(n_mix)]
    w_outs = [w_out] * n_mix
    return pl.pallas_call(
        functools.partial(_post_kernel, n_mix=n_mix),
        grid=(m // tm,),
        in_specs=([row_spec(a) for a in mixes] + w_specs
                  + [row_spec(x), pl.BlockSpec((1, d), lambda i: (0, 0)), row_spec(p),
                     full_spec(w_gate), full_spec(w_proj), pl.BlockSpec((1, d), lambda i: (0, 0))]),
        out_specs=[row_spec(x), row_spec(x)],
        out_shape=[jax.ShapeDtypeStruct((m, d), F32), jax.ShapeDtypeStruct((m, d), BF16)],
        compiler_params=pltpu.CompilerParams(dimension_semantics=("parallel",),
                                             vmem_limit_bytes=VMEM_LIMIT_BYTES),
        name="post",
    )(*mixes, *w_outs, x, post_g.reshape(1, d), p, w_gate, w_proj, next_g.reshape(1, d))


def _hgrn_kernel(pm_ref, lb_ref, nw_ref, sel_ref, out_ref, s_ref, kc_ref, *, first, n_heads, width):
    L = CHUNK

    @pl.when(pl.program_id(1) == 0)
    def _():
        s_ref[...] = jnp.zeros_like(s_ref)

    row = lax.broadcasted_iota(jnp.int32, (L, L), 0)
    col = lax.broadcasted_iota(jnp.int32, (L, L), 1)
    tril = _tril_bf16()
    diag_mask = jnp.logical_and(col // SUB == row // SUB, col <= row)
    nw = nw_ref[...]
    sel = sel_ref[...]

    def head_chain(sq, h):
        ls = slice(h * LANES, (h + 1) * LANES)
        qp = pm_ref[sq, :, h * LANES:(h + 1) * LANES]
        z = pm_ref[sq, :, width + h * LANES:width + (h + 1) * LANES]
        v = pm_ref[sq, :, 2 * width + h * LANES:2 * width + (h + 1) * LANES].astype(BF16)

        e = jnp.exp(-jnp.abs(z))
        inv = 1.0 / (1.0 + e)
        pos = z >= 0.0
        sig = jnp.where(pos, inv, e * inv)
        nsig = jnp.where(pos, e * inv, inv)
        if first:
            lf = jnp.minimum(z, 0.0) - jnp.log(1.0 + e)
            k = nsig
        else:
            lb = lb_ref[:, ls]
            lf = jnp.log(lb + (1.0 - lb) * sig)
            k = (1.0 - lb) * nsig
        q = _silu(qp)
        c = _cumsum_rows(tril, lf)
        yield
        c = c * LOG2E
        kc_ref[sq, h, 0] = k
        kc_ref[sq, h, 1] = c

        att_rows = [jnp.zeros((SUB, L), F32)]
        for i in range(1, L // SUB):
            ref_c = kc_ref[sq, h, 1, i * SUB:i * SUB + 1, :]
            kt = k[0:i * SUB] * jnp.exp2(ref_c - c[0:i * SUB])
            kt = jnp.concatenate([kt, jnp.zeros((L - i * SUB, LANES), F32)], axis=0).astype(BF16)
            qt = jnp.concatenate([q[i * SUB:(i + 1) * SUB] * jnp.exp2(c[i * SUB:(i + 1) * SUB] - ref_c),
                                  jnp.zeros((BF16_ROWS - SUB, LANES), F32)], axis=0).astype(BF16)
            att_rows.append(_dot_nt(qt, kt)[0:SUB])

        p_rows = []
        for i in range(L // SUB):
            q_i = q[i * SUB:(i + 1) * SUB]
            c_i = c[i * SUB:(i + 1) * SUB]
            pieces = []
            for j in range(SUB):
                k_row = kc_ref[sq, h, 0, i * SUB + j:i * SUB + j + 1, :]
                c_row = kc_ref[sq, h, 1, i * SUB + j:i * SUB + j + 1, :]
                dec = jnp.exp2(jnp.minimum(c_i - c_row, 0.0))
                pieces.append(q_i * dec * k_row)
            p_rows.append(jnp.concatenate(pieces, axis=1))
        att_diag = _dot(jnp.concatenate(p_rows, axis=0).astype(BF16), sel)

        st = s_ref[sq, h]
        qs = _dot_nt((q * jnp.exp2(c)).astype(BF16), st.astype(BF16))
        c_last = kc_ref[sq, h, 1, L - 1:L, :]
        upd = _dot_tn(v, (k * jnp.exp2(c_last - c)).astype(BF16))
        yield
        att = jnp.concatenate(att_rows, axis=0) + jnp.where(diag_mask, att_diag[:, :L], 0.0)
        o = qs + _dot(att.astype(BF16), v)
        s_ref[sq, h] = jnp.exp2(c_last) * st + upd
        yield
        gb = pm_ref[sq, :, 3 * width + h * LANES:3 * width + (h + 1) * LANES]
        ms = jnp.mean(o * o, axis=-1, keepdims=True)
        out_ref[sq, :, ls] = (o * lax.rsqrt(ms + NORM_EPS) * nw * _silu(gb)).astype(BF16)

    _run_lockstep([functools.partial(head_chain, sq, h) for sq in range(SEQS_PER_STEP) for h in range(n_heads)],
                  HGRN_LOCKSTEP)


def _hgrn(proj_main, col_block, lb, norm_w, first, n_chunks, width):
    batch, seq, _ = proj_main.shape
    n_heads = width // LANES
    j = jnp.arange(SUB * LANES) // LANES
    s = jnp.arange(LANES)
    sel = ((s[None, :] % SUB == j[:, None]) & (s[None, :] < CHUNK)).astype(BF16)
    args = [proj_main, lb.reshape(1, width), norm_w.reshape(1, LANES), sel]
    return pl.pallas_call(
        functools.partial(_hgrn_kernel, first=first, n_heads=n_heads, width=width),
        grid=(batch // SEQS_PER_STEP, n_chunks),
        in_specs=[_seq_block(4 * width, n_chunks, col_block)] + [_const_block(a) for a in args[1:]],
        out_specs=_seq_block(width, n_chunks),
        out_shape=jax.ShapeDtypeStruct((batch, seq, width), BF16),
        scratch_shapes=[pltpu.VMEM((SEQS_PER_STEP, n_heads, LANES, LANES), F32),
                        pltpu.VMEM((SEQS_PER_STEP, n_heads, 2, CHUNK, LANES), F32)],
        compiler_params=pltpu.CompilerParams(dimension_semantics=("parallel", "arbitrary"),
                                             vmem_limit_bytes=VMEM_LIMIT_BYTES),
        name="hgrn2",
    )(*args)


def _rwkv_kernel(*refs, has_vres, n_pairs, width):
    (pm_ref, ps_ref, mum_ref, mus_ref, w0_ref, w2_ref, a0_ref, a2_ref, kk_ref, ka_ref, rk_ref,
     lnw_ref, lnb_ref) = refs[:13]
    if has_vres:
        vf_ref, v0_ref, v2_ref, out_ref, s_ref, cm_ref, cs_ref, fm_ref = refs[13:]
        vfo_ref = None
    else:
        out_ref, vfo_ref, s_ref, cm_ref, cs_ref, fm_ref = refs[13:]
    L = CHUNK

    @pl.when(pl.program_id(1) == 0)
    def _():
        s_ref[...] = jnp.zeros_like(s_ref)
        cm_ref[...] = jnp.zeros_like(cm_ref)
        cs_ref[...] = jnp.zeros_like(cs_ref)

    row1 = lax.broadcasted_iota(jnp.int32, (L, 1), 0)

    def token_shift(x, carry_row, mu):
        prev = jnp.where(row1 == 0, carry_row, pltpu.roll(x, 1, 0))
        return x + (prev - x) * mu

    tril = _tril_bf16()

    for sq in range(SEQS_PER_STEP):
        ps = ps_ref[sq]
        pss = token_shift(ps, cs_ref[sq, 0:1, :], mus_ref[...])
        cs_ref[sq, 0:1, :] = ps[L - 1:L, :]
        wd_t = jnp.tanh(pss[:, 0:LANES]).astype(BF16)
        ad = pss[:, LANES:2 * LANES].astype(BF16)
        lw_all = -math.exp(-0.5) * _sigmoid(w0_ref[...] + _dot(wd_t, w2_ref[...]))
        fm_ref[sq, 0] = lw_all * LOG2E
        fm_ref[sq, 1] = _cumsum_rows(tril, lw_all) * LOG2E
        fm_ref[sq, 2] = _sigmoid(a0_ref[...] + _dot(ad, a2_ref[...]))
        if has_vres:
            vd = pss[:, 2 * LANES:3 * LANES].astype(BF16)
            fm_ref[sq, 3] = _sigmoid(v0_ref[...] + _dot(vd, v2_ref[...]))

    _, m0, m1 = _lane_masks()
    strict2, incl2 = _pair_time_masks()
    rr = lax.broadcasted_iota(jnp.int32, (LANES, LANES), 0) // RWKV_HEAD_DIM
    cc = lax.broadcasted_iota(jnp.int32, (LANES, LANES), 1) // RWKV_HEAD_DIM
    bd_mask = (rr == cc).astype(F32)
    g128 = bd_mask.astype(BF16)
    inv_hd = 1.0 / RWKV_HEAD_DIM

    def stack_heads(x):
        return jnp.concatenate([x * m0, x * m1], axis=0)

    def pair_chain(sq, j):
        ls = slice(j * LANES, (j + 1) * LANES)

        def load(q):
            sl = slice(q * width + j * LANES, q * width + (j + 1) * LANES)
            x = pm_ref[sq, :, sl]
            y = token_shift(x, cm_ref[sq, 0:1, sl], mum_ref[:, sl])
            cm_ref[sq, 0:1, sl] = x[L - 1:L, :]
            return y

        r, k, v = load(0), load(1), load(2)
        lw, c, icl = fm_ref[sq, 0, :, ls], fm_ref[sq, 1, :, ls], fm_ref[sq, 2, :, ls]
        if has_vres:
            v = v + (vf_ref[sq, :, ls] - v) * fm_ref[sq, 3, :, ls]
        else:
            vfo_ref[sq, :, ls] = v
        kkr = k * kk_ref[:, ls]
        k2 = k * (1.0 + (icl - 1.0) * ka_ref[:, ls])
        sums = _dot(jnp.concatenate([kkr * kkr, r * k2 * rk_ref[:, ls]], axis=0).astype(BF16), g128)
        yield
        kkn = kkr * lax.rsqrt(sums[0:L] + 1e-12)
        bonus = sums[L:2 * L] * v
        b = kkn * icl
        c_last = c[L - 1:L, :]
        e_neg = jnp.exp2(-c)
        e_tail = jnp.exp2(c_last - c)
        a_t = -kkn * jnp.exp2(c - lw)
        r_t = r * jnp.exp2(c)
        lhs = jnp.concatenate([a_t, r_t], axis=0).astype(BF16)
        rhs_t = jnp.concatenate([stack_heads(b * e_neg), stack_heads(k2 * e_neg)], axis=0).astype(BF16)
        mm = _dot_nt(lhs, rhs_t)
        st = s_ref[sq, j]
        sr = _dot_nt(lhs, st.astype(BF16))
        yield
        m_ab = jnp.where(strict2, mm[0:L, 0:LANES], 0.0)
        m_ak = jnp.where(strict2, mm[0:L, LANES:2 * LANES], 0.0)
        p_rb = jnp.where(incl2, mm[L:2 * L, 0:LANES], 0.0)
        p_rk = jnp.where(incl2, mm[L:2 * L, LANES:2 * LANES], 0.0)
        v_bd = stack_heads(v).astype(BF16)
        rhs = sr[0:L] + _dot(m_ak.astype(BF16), v_bd)
        for step in _neumann_inverse_steps(stack_heads(m_ab)):
            if step is None:
                yield
            else:
                t_bd, = step
        t_cat = (t_bd[0:L] + t_bd[L:2 * L]).astype(BF16)
        u = _dot(t_cat, stack_heads(rhs).astype(BF16))
        yield
        u_bd = stack_heads(u).astype(BF16)
        o = sr[L:2 * L] + _dot(jnp.concatenate([p_rb, p_rk], axis=1).astype(BF16),
                               jnp.concatenate([u_bd, v_bd], axis=0))
        upd = _dot_tn(jnp.concatenate([u, v], axis=0).astype(BF16),
                      jnp.concatenate([b * e_tail, k2 * e_tail], axis=0).astype(BF16))
        yield
        s_ref[sq, j] = (jnp.exp2(c_last) * st + upd) * bd_mask
        mean = _dot(o.astype(BF16), g128) * inv_hd
        yield
        dlt = o - mean
        var = _dot((dlt * dlt).astype(BF16), g128) * inv_hd
        yield
        g = load(3)
        o_n = dlt * lax.rsqrt(var + RWKV_GN_EPS) * lnw_ref[:, ls] + lnb_ref[:, ls]
        out_ref[sq, :, ls] = ((o_n + bonus) * _silu(g)).astype(BF16)

    _run_lockstep([functools.partial(pair_chain, sq, j) for sq in range(SEQS_PER_STEP) for j in range(n_pairs)],
                  RWKV_LOCKSTEP)


def _rwkv(proj_main, proj_small, prm, v_first, n_chunks, width):
    batch, seq, _ = proj_main.shape
    n_pairs = width // LANES
    has_vres = v_first is not None
    ns = proj_small.shape[2]
    consts = [prm["mu_main"], prm["mu_small"], prm["w0"], prm["w2"], prm["a0"], prm["a2"],
              prm["k_k"], prm["k_a"], prm["r_k"], prm["lnx_w"], prm["lnx_b"]]
    args = [proj_main, proj_small] + consts
    in_specs = [_seq_block(4 * width, n_chunks), _seq_block(ns, n_chunks)] + [_const_block(a) for a in consts]
    if has_vres:
        args += [v_first, prm["v0"], prm["v2"]]
        in_specs += [_seq_block(width, n_chunks), _const_block(prm["v0"]), _const_block(prm["v2"])]
        out_specs = _seq_block(width, n_chunks)
        out_shape = jax.ShapeDtypeStruct((batch, seq, width), BF16)
    else:
        out_specs = [_seq_block(width, n_chunks), _seq_block(width, n_chunks)]
        out_shape = [jax.ShapeDtypeStruct((batch, seq, width), BF16), jax.ShapeDtypeStruct((batch, seq, width), F32)]
    return pl.pallas_call(
        functools.partial(_rwkv_kernel, has_vres=has_vres, n_pairs=n_pairs, width=width),
        grid=(batch // SEQS_PER_STEP, n_chunks),
        in_specs=in_specs,
        out_specs=out_specs,
        out_shape=out_shape,
        scratch_shapes=[pltpu.VMEM((SEQS_PER_STEP, n_pairs, LANES, LANES), F32),
                        pltpu.VMEM((SEQS_PER_STEP, CARRY_ROWS, 4 * width), F32),
                        pltpu.VMEM((SEQS_PER_STEP, CARRY_ROWS, ns), F32),
                        pltpu.VMEM((SEQS_PER_STEP, 4, CHUNK, width), F32)],
        compiler_params=pltpu.CompilerParams(dimension_semantics=("parallel", "arbitrary"),
                                             vmem_limit_bytes=VMEM_LIMIT_BYTES),
        name="rwkv7",
    )(*args)


def _gdn_kernel(qkv_ref, z_ref, sm_ref, cw_ref, al_ref, dtb_ref, nw_ref, out_ref,
                s_ref, carry_ref, act_ref, gate_ref, grow_ref, *, n_qk):
    L = CHUNK
    n_v = 2 * n_qk

    @pl.when(pl.program_id(1) == 0)
    def _():
        s_ref[...] = jnp.zeros_like(s_ref)
        carry_ref[...] = jnp.zeros_like(carry_ref)

    q_scale = GDN_HEAD_DIM ** -0.5
    tril = _tril_bf16()

    for sq in range(SEQS_PER_STEP):
        for gi in range(4 * n_qk):
            sl = slice(gi * LANES, (gi + 1) * LANES)
            x = qkv_ref[sq, :, sl]
            ext = jnp.concatenate([carry_ref[sq, :, sl], x], axis=0)
            acc = x * cw_ref[GDN_CONV - 1:GDN_CONV, sl]
            for d in range(1, GDN_CONV):
                acc = acc + pltpu.roll(ext, d, 0)[CARRY_ROWS:] * cw_ref[GDN_CONV - 1 - d:GDN_CONV - d, sl]
            carry_ref[sq, :, sl] = x[L - CARRY_ROWS:L, :]
            y = _silu(acc)
            if gi < 2 * n_qk:
                y = y * lax.rsqrt(jnp.sum(y * y, axis=-1, keepdims=True) + 1e-6)
                if gi < n_qk:
                    y = y * q_scale
            act_ref[sq, :, sl] = y

        sm = sm_ref[sq]
        beta = _sigmoid(sm)
        g_log = -jnp.exp(al_ref[...]) * _softplus(sm + dtb_ref[...])
        g2 = _cumsum_rows(tril, g_log) * LOG2E
        eg = jnp.exp2(g2)
        beg = beta * pltpu.roll(eg, LANES - n_v, 1)
        grow_ref[sq] = jnp.concatenate([g2, jnp.zeros((LANES - L, LANES), F32)], axis=0).T
        for h in range(n_v):
            gate_ref[sq, h, 0] = jnp.broadcast_to(g2[:, n_v + h:n_v + h + 1], (L, LANES))
            gate_ref[sq, h, 1] = jnp.broadcast_to(beta[:, h:h + 1], (L, LANES))
            gate_ref[sq, h, 2] = jnp.broadcast_to(eg[:, n_v + h:n_v + h + 1], (L, LANES))
            gate_ref[sq, h, 3] = jnp.broadcast_to(beg[:, h:h + 1], (L, LANES))

    lane, m0, m1 = _lane_masks()
    low = lane < LANES // 2
    strict2, incl2 = _pair_time_masks()
    nw = nw_ref[...]

    def pair_chain(sq, p):
        q_h = act_ref[sq, :, p * LANES:(p + 1) * LANES]
        k_h = act_ref[sq, :, (n_qk + p) * LANES:(n_qk + p + 1) * LANES]
        q_b, k_b = q_h.astype(BF16), k_h.astype(BF16)
        heads = (2 * p, 2 * p + 1)
        mm = _dot_nt(jnp.concatenate([q_b, k_b], axis=0), jnp.concatenate([k_b, k_b], axis=0))
        states = [s_ref[sq, h] for h in heads]
        states_b = [s.astype(BF16) for s in states]
        ks = [_dot((gate_ref[sq, h, 3] * k_h).astype(BF16), states_b[e]) for e, h in enumerate(heads)]
        yield
        g_full = [gate_ref[sq, h, 0] for h in heads]
        b_full = [gate_ref[sq, h, 1] for h in heads]
        g_cat = jnp.where(low, g_full[0], g_full[1])
        b_cat = jnp.where(low, b_full[0], b_full[1])
        g_row = (grow_ref[sq, n_v + heads[0]:n_v + heads[0] + 1, :]
                 + pltpu.roll(grow_ref[sq, n_v + heads[1]:n_v + heads[1] + 1, :], LANES // 2, 1))
        dm = jnp.where(incl2, jnp.exp2(jnp.minimum(g_cat - g_row, 0.0)), 0.0)
        n_cat = jnp.where(strict2, -(b_cat * mm[L:2 * L] * dm), 0.0)
        qk_dm = mm[0:L] * dm
        rhs = []
        for e, h in enumerate(heads):
            v_h = act_ref[sq, :, (2 * n_qk + h) * LANES:(2 * n_qk + h + 1) * LANES]
            rhs.append(b_full[e] * v_h - ks[e])
        rhs = jnp.concatenate(rhs, axis=0).astype(BF16)
        for step in _neumann_inverse_steps(jnp.concatenate([n_cat * m0, n_cat * m1], axis=0)):
            if step is None:
                yield
            else:
                t_bd, = step
        new_v = _dot(t_bd.astype(BF16), rhs)
        yield
        nv_stack = new_v.astype(BF16)
        outs, upds, g_lasts = [], [], []
        for e, h in enumerate(heads):
            mask = m0 if e == 0 else m1
            lhs = jnp.concatenate([q_h * gate_ref[sq, h, 2], qk_dm * mask], axis=1).astype(BF16)
            outs.append(_dot(lhs, jnp.concatenate([states_b[e], nv_stack], axis=0)))
            g_lasts.append(g_full[e][L - 1:L, :])
            k_tail = (k_h * jnp.exp2(g_lasts[e] - g_full[e])).astype(BF16)
            upds.append(_dot_tn(k_tail, nv_stack[e * L:(e + 1) * L]))
        yield
        for e, h in enumerate(heads):
            s_ref[sq, h] = jnp.exp2(g_lasts[e]) * states[e] + upds[e]
            o = outs[e]
            z = z_ref[sq, :, h * LANES:(h + 1) * LANES]
            ms = jnp.mean(o * o, axis=-1, keepdims=True)
            out_ref[sq, :, h * LANES:(h + 1) * LANES] = (o * lax.rsqrt(ms + NORM_EPS) * nw * _silu(z)).astype(BF16)

    _run_lockstep([functools.partial(pair_chain, sq, p) for sq in range(SEQS_PER_STEP) for p in range(n_qk)],
                  GDN_LOCKSTEP)


def _gdn(proj_main, proj_small, conv_w, a_log, dt_bias, norm_w, n_chunks, n_qk):
    batch, seq, _ = proj_main.shape
    n_v = 2 * n_qk
    qkv_w = 4 * n_qk * LANES
    z_w = n_v * LANES
    cw = jnp.concatenate([conv_w, jnp.zeros((CARRY_ROWS - GDN_CONV, qkv_w), F32)], axis=0)
    al = jnp.zeros((1, LANES), F32).at[0, n_v:2 * n_v].set(a_log)
    dtb = jnp.zeros((1, LANES), F32).at[0, n_v:2 * n_v].set(dt_bias)
    nw = norm_w.reshape(1, LANES)
    return pl.pallas_call(
        functools.partial(_gdn_kernel, n_qk=n_qk),
        grid=(batch // SEQS_PER_STEP, n_chunks),
        in_specs=[_seq_block(qkv_w, n_chunks), _seq_block(z_w, n_chunks, qkv_w // z_w), _seq_block(LANES, n_chunks),
                  _const_block(cw), _const_block(al), _const_block(dtb), _const_block(nw)],
        out_specs=_seq_block(z_w, n_chunks),
        out_shape=jax.ShapeDtypeStruct((batch, seq, z_w), BF16),
        scratch_shapes=[pltpu.VMEM((SEQS_PER_STEP, n_v, LANES, LANES), F32),
                        pltpu.VMEM((SEQS_PER_STEP, CARRY_ROWS, qkv_w), F32),
                        pltpu.VMEM((SEQS_PER_STEP, CHUNK, qkv_w), F32),
                        pltpu.VMEM((SEQS_PER_STEP, n_v, 4, CHUNK, LANES), F32),
                        pltpu.VMEM((SEQS_PER_STEP, LANES, LANES), F32)],
        compiler_params=pltpu.CompilerParams(dimension_semantics=("parallel", "arbitrary"),
                                             vmem_limit_bytes=VMEM_LIMIT_BYTES),
        name="gdn",
    )(proj_main, proj_main, proj_small, cw, al, dtb, nw)


def _pad_rows(w, rows):
    return jnp.concatenate([w, jnp.zeros((rows - w.shape[0], w.shape[1]), w.dtype)], axis=0)


def _even_params(j, width, ev_w_in, ev_mu, ev_w0, ev_w2, ev_a0, ev_a2, ev_k_k, ev_k_a, ev_r_k, ev_lnx_w, ev_lnx_b,
                 vr_w_down, vr_mu, vr_v0, vr_v2):
    d = ev_w_in.shape[1]
    w_in, mu = ev_w_in[j], ev_mu[j]
    rw = 4 * width
    lo_a, lo_b = rw + DECAY_RANK, rw + DECAY_RANK + ICL_RANK
    w_main = jnp.concatenate([w_in[:, :rw], w_in[:, lo_b:]], axis=1).astype(BF16)
    w_small = jnp.zeros((d, 3 * LANES), F32)
    w_small = w_small.at[:, 0:DECAY_RANK].set(w_in[:, rw:lo_a])
    w_small = w_small.at[:, LANES:LANES + ICL_RANK].set(w_in[:, lo_a:lo_b])
    mu_small = jnp.zeros((1, 3 * LANES), F32)
    mu_small = mu_small.at[0, 0:DECAY_RANK].set(mu[rw:lo_a])
    mu_small = mu_small.at[0, LANES:LANES + ICL_RANK].set(mu[lo_a:lo_b])
    row = lambda a: a.reshape(1, width)
    prm = dict(mu_main=mu[:rw].reshape(1, rw), w0=row(ev_w0[j]), w2=_pad_rows(ev_w2[j], LANES).astype(BF16),
               a0=row(ev_a0[j]), a2=_pad_rows(ev_a2[j], LANES).astype(BF16), k_k=row(ev_k_k[j]), k_a=row(ev_k_a[j]),
               r_k=row(ev_r_k[j]), lnx_w=row(ev_lnx_w[j]), lnx_b=row(ev_lnx_b[j]))
    if j > 0:
        w_small = w_small.at[:, 2 * LANES:2 * LANES + VRES_RANK].set(vr_w_down[j - 1])
        mu_small = mu_small.at[0, 2 * LANES:2 * LANES + VRES_RANK].set(vr_mu[j - 1])
        prm["v0"] = row(vr_v0[j - 1])
        prm["v2"] = _pad_rows(vr_v2[j - 1], LANES).astype(BF16)
    prm["mu_small"] = mu_small
    return w_main, w_small.astype(BF16), prm


def kernel(x, p, pre_g, post_g, ple_w_proj, ple_w_gate, ev_w_in, ev_mu, ev_w0, ev_w2, ev_a0, ev_a2, ev_k_k, ev_k_a, ev_r_k, ev_lnx_w, ev_lnx_b, vr_w_down, vr_mu, vr_v0, vr_v2, hg_lb, hg_norm_w, ev_w_out, od_w_in, od_conv_w, od_a_log, od_dt_bias, od_norm_w, od_w_out):
    batch, seq, d = x.shape
    assert batch % SEQS_PER_STEP == 0 and seq % CHUNK == 0
    depth = pre_g.shape[0]
    m = batch * seq
    n_chunks = seq // CHUNK
    width = ev_w0.shape[1]
    n_v = od_a_log.shape[1]
    n_qk = n_v // 2
    per_seq = lambda a: a.reshape(batch, seq, a.shape[-1])
    flat = lambda a: a.reshape(m, a.shape[-1])

    lb_all = jnp.cumsum(jax.nn.softmax(hg_lb.astype(F32), axis=0), axis=0)
    lb_all = lb_all - lb_all[0]

    xf = x.reshape(m, d)
    v_first = None
    h_in, g_in = xf, pre_g[0]
    for i in range(depth):
        j = i // 2
        if i % 2 == 0:
            w_main, w_small, prm = _even_params(j, width, ev_w_in, ev_mu, ev_w0, ev_w2, ev_a0, ev_a2, ev_k_k, ev_k_a,
                                                ev_r_k, ev_lnx_w, ev_lnx_b, vr_w_down, vr_mu, vr_v0, vr_v2)
            proj_main, proj_small = _norm_proj(h_in, g_in, w_main, w_small)
            proj_main, proj_small = per_seq(proj_main), per_seq(proj_small)
            if j == 0:
                mix_a, v_first = _rwkv(proj_main, proj_small, prm, None, n_chunks, width)
            else:
                mix_a = _rwkv(proj_main, proj_small, prm, v_first, n_chunks, width)
            mix_b = _hgrn(proj_main, 1, lb_all[j], hg_norm_w[j], j == 0, n_chunks, width)
            mixes, w_out = [flat(mix_a), flat(mix_b)], ev_w_out[j].astype(BF16)
        else:
            w_in = od_w_in[j]
            n_main = (4 * n_qk + n_v) * LANES
            w_small = jnp.zeros((d, LANES), F32).at[:, 0:2 * n_v].set(w_in[:, n_main:])
            proj_main, proj_small = _norm_proj(h_in, g_in, w_in[:, :n_main].astype(BF16), w_small.astype(BF16))
            mix = _gdn(per_seq(proj_main), per_seq(proj_small), od_conv_w[j], od_a_log[j], od_dt_bias[j],
                       od_norm_w[j], n_chunks, n_qk)
            mixes, w_out = [flat(mix)], od_w_out[j].astype(BF16)
        xf, h_in = _post(mixes, w_out, xf, post_g[i], p[i].reshape(m, -1), ple_w_gate[i].astype(BF16),
                         ple_w_proj[i].astype(BF16), pre_g[min(i + 1, depth - 1)])
        g_in = None
    return xf.reshape(batch, seq, d)
```
